```python
import jax, jax.numpy as jnp
from jax import lax
import numpy as np

D_MODEL = 1024
BATCH = 8
SEQ = 2048
DEPTH = 4
DEC_BATCH = 4
DEC_SEQ = 4096
PAST_LEN = 128

HEAD_DIM = 64
H_A = 8
HKV_A = 2
GQA = H_A // HKV_A
H_B = 8
WINDOW = 128
WBLK = 128
ROPE_THETA = 10000.0
GRID_W = 64
NA_ROWS = 8
NA_COLS = 16
QCB = 16
KCB = 32
N_CBLK = GRID_W // QCB
D_FF = 2816
N_EXPERTS = 8
TOP_K = 2
N_DENSE = (DEPTH + 1) // 2
N_MOE = DEPTH // 2
ALPHA = (2.0 * DEPTH) ** 0.25
BETA = (8.0 * DEPTH) ** -0.25
LN_EPS = 1e-5
NEG = -1e30
QA_W = H_A * HEAD_DIM
KVA_W = HKV_A * HEAD_DIM
B_W = H_B * HEAD_DIM
IN_W = QA_W + 2 * KVA_W + 3 * B_W
SPLITS = [QA_W, QA_W + KVA_W, QA_W + 2 * KVA_W, QA_W + 2 * KVA_W + B_W, QA_W + 2 * KVA_W + 2 * B_W]

kernel_name = "hymba_window_natten_deepnorm_encoder"


def _layernorm(x, g, b):
    xf = x.astype(jnp.float32)
    mu = jnp.mean(xf, axis=-1, keepdims=True)
    xc = xf - mu
    var = jnp.mean(xc * xc, axis=-1, keepdims=True)
    return (xc * lax.rsqrt(var + LN_EPS) * g.astype(jnp.float32) + b.astype(jnp.float32)).astype(x.dtype)


def _rmsnorm(x, g):
    xf = x.astype(jnp.float32)
    ms = jnp.mean(xf * xf, axis=-1, keepdims=True)
    return (xf * lax.rsqrt(ms + LN_EPS) * g.astype(jnp.float32)).astype(x.dtype)


def _rope(x):
    S, d = x.shape[1], x.shape[-1]
    half = d // 2
    inv = 1.0 / (ROPE_THETA ** (jnp.arange(half, dtype=jnp.float32) / half))
    ang = jnp.arange(S, dtype=jnp.float32)[:, None] * inv[None, :]
    cos = jnp.cos(ang)[:, None, :]
    sin = jnp.sin(ang)[:, None, :]
    xf = x.astype(jnp.float32)
    x1, x2 = xf[..., :half], xf[..., half:]
    return jnp.concatenate([x1 * cos - x2 * sin, x2 * cos + x1 * sin], axis=-1).astype(x.dtype)


def _window_attn(q, k, v, sink):
    B, S = q.shape[0], q.shape[1]
    nb = S // WBLK
    qb = q.reshape(B, nb, WBLK, HKV_A, GQA, HEAD_DIM)
    pad = ((0, 0), (1, 1), (0, 0), (0, 0), (0, 0))
    kp = jnp.pad(k.reshape(B, nb, WBLK, HKV_A, HEAD_DIM), pad)
    vp = jnp.pad(v.reshape(B, nb, WBLK, HKV_A, HEAD_DIM), pad)
    kw = jnp.concatenate([kp[:, :-2], kp[:, 1:-1], kp[:, 2:]], axis=2)
    vw = jnp.concatenate([vp[:, :-2], vp[:, 1:-1], vp[:, 2:]], axis=2)
    s = jnp.einsum('bnqkgd,bnjkd->bnkgqj', qb, kw).astype(jnp.float32) * (HEAD_DIM ** -0.5)
    rel = np.arange(3 * WBLK)[None, :] - WBLK - np.arange(WBLK)[:, None]
    band = np.abs(rel) <= WINDOW
    kpos = (np.arange(nb)[:, None] - 1) * WBLK + np.arange(3 * WBLK)[None, :]
    inr = (kpos >= 0) & (kpos < S)
    mask = band[None, :, :] & inr[:, None, :]
    s = jnp.where(mask[None, :, None, None, :, :], s, NEG)
    sink_l = jnp.broadcast_to(sink.astype(jnp.float32).reshape(HKV_A, GQA)[None, None, :, :, None, None],
                              s.shape[:-1] + (1,))
    p = jax.nn.softmax(jnp.concatenate([s, sink_l], axis=-1), axis=-1)[..., :-1]
    o = jnp.einsum('bnkgqj,bnjkd->bnqkgd', p.astype(v.dtype), vw)
    return o.reshape(B, S, QA_W)


def _neighborhood_attn(q, k, v, rpb):
    B, S = q.shape[0], q.shape[1]
    rows = S // GRID_W
    kr = min(NA_ROWS, rows)
    r = np.arange(rows)
    rstart = np.clip(r - kr // 2, 0, rows - kr)
    row_idx = rstart[:, None] + np.arange(kr)[None, :]
    m = np.arange(N_CBLK)
    cbstart = np.clip(m * QCB - NA_COLS // 2, 0, GRID_W - KCB)
    col_blk = cbstart[:, None] + np.arange(KCB)[None, :]
    c = m[:, None] * QCB + np.arange(QCB)[None, :]
    cstart = np.clip(c - NA_COLS // 2, 0, GRID_W - NA_COLS)
    kc = col_blk[:, None, :]
    cmask = (kc >= cstart[..., None]) & (kc < cstart[..., None] + NA_COLS)
    qg = q.reshape(B, rows, N_CBLK, QCB, H_B, HEAD_DIM)
    kg = k.reshape(B, rows, GRID_W, H_B, HEAD_DIM)
    vg = v.reshape(B, rows, GRID_W, H_B, HEAD_DIM)
    ri = row_idx[:, None, :, None]
    ci = col_blk[None, :, None, :]
    kblk = kg[:, ri, ci]
    vblk = vg[:, ri, ci]
    s = jnp.einsum('brmqhd,brmnjhd->brmhqnj', qg, kblk).astype(jnp.float32) * (HEAD_DIM ** -0.5)
    dr = row_idx - r[:, None] + (NA_ROWS - 1)
    dc = np.clip(kc - c[..., None] + (NA_COLS - 1), 0, 2 * NA_COLS - 2)
    bias = rpb.astype(jnp.float32)[:, dr[:, None, None, :, None], dc[None, :, :, None, :]]
    bias = jnp.transpose(bias, (1, 2, 0, 3, 4, 5))
    s = jnp.where(cmask[None, :, None, :, None, :], s + bias[None], NEG)
    s = s.reshape(B, rows, N_CBLK, H_B, QCB, kr * KCB)
    p = jax.nn.softmax(s, axis=-1)
    o = jnp.einsum('brmhqn,brmnhd->brmqhd', p.astype(v.dtype),
                   vblk.reshape(B, rows, N_CBLK, kr * KCB, H_B, HEAD_DIM))
    return o.reshape(B, S, B_W)


def _swiglu(x, wg, wu, wd):
    return (jax.nn.silu(x @ wg) * (x @ wu)) @ wd


def _moe(x, router, e_gate, e_up, e_down):
    logits = (x @ router).astype(jnp.float32)
    topv, topi = lax.top_k(logits, TOP_K)
    w = jax.nn.softmax(topv, axis=-1)
    gates = jnp.sum(jax.nn.one_hot(topi, N_EXPERTS, dtype=jnp.float32) * w[..., None], axis=-2)
    out = jnp.zeros_like(x)
    for e in range(N_EXPERTS):
        out = out + gates[..., e:e + 1].astype(x.dtype) * _swiglu(x, e_gate[e], e_up[e], e_down[e])
    return out


def _trunk(x, emb_ln_g, emb_ln_b, w_in, attn_sink, na_rpb, gnorm_a, gnorm_b, w_o, ln1_g, ln1_b,
           ffn_gate, ffn_up, ffn_down, router, exp_gate, exp_up, exp_down, ln2_g, ln2_b):
    B, S, _ = x.shape
    x = _layernorm(x, emb_ln_g, emb_ln_b)
    for l in range(DEPTH):
        h = x @ w_in[l]
        qa, ka, va, qb, kb, vb = jnp.split(h, SPLITS, axis=-1)
        qa = _rope(qa.reshape(B, S, H_A, HEAD_DIM))
        ka = _rope(ka.reshape(B, S, HKV_A, HEAD_DIM))
        va = va.reshape(B, S, HKV_A, HEAD_DIM)
        oa = _window_attn(qa, ka, va, attn_sink[l])
        ob = _neighborhood_attn(qb.reshape(B, S, H_B, HEAD_DIM), kb.reshape(B, S, H_B, HEAD_DIM),
                                vb.reshape(B, S, H_B, HEAD_DIM), na_rpb[l])
        mix = jnp.concatenate([_rmsnorm(oa, gnorm_a[l]), _rmsnorm(ob, gnorm_b[l])], axis=-1) @ w_o[l]
        x = _layernorm(ALPHA * x + mix, ln1_g[l], ln1_b[l])
        if l % 2 == 0:
            i = l // 2
            f = _swiglu(x, ffn_gate[i], ffn_up[i], ffn_down[i])
        else:
            i = l // 2
            f = _moe(x, router[i], exp_gate[i], exp_up[i], exp_down[i])
        x = _layernorm(ALPHA * x + f, ln2_g[l], ln2_b[l])
    return x


def setup_inputs(seed: int = 0) -> dict:
    key = jax.random.key(seed)
    ks = jax.random.split(key, 24)
    nrm = lambda k, shape, s: jax.random.normal(k, shape, jnp.float32) * s
    D = D_MODEL
    return {
        "x_prompt": nrm(ks[0], (BATCH, SEQ, D), 1.0),
        "x_sample": nrm(ks[1], (DEC_BATCH, DEC_SEQ, D), 1.0),
        "emb_ln_g": 1.0 + nrm(ks[2], (D,), 0.02),
        "emb_ln_b": nrm(ks[3], (D,), 0.02),
        "w_in": nrm(ks[4], (DEPTH, D, IN_W), D ** -0.5),
        "attn_sink": nrm(ks[5], (DEPTH, H_A), 0.5),
        "na_rpb": nrm(ks[6], (DEPTH, H_B, 2 * NA_ROWS - 1, 2 * NA_COLS - 1), 0.1),
        "gnorm_a": 1.0 + nrm(ks[7], (DEPTH, QA_W), 0.02),
        "gnorm_b": 1.0 + nrm(ks[8], (DEPTH, B_W), 0.02),
        "w_o": nrm(ks[9], (DEPTH, QA_W + B_W, D), BETA * (QA_W + B_W) ** -0.5),
        "ln1_g": 1.0 + nrm(ks[10], (DEPTH, D), 0.02),
        "ln1_b": nrm(ks[11], (DEPTH, D), 0.02),
        "ffn_gate": nrm(ks[12], (N_DENSE, D, D_FF), D ** -0.5),
        "ffn_up": nrm(ks[13], (N_DENSE, D, D_FF), D ** -0.5),
        "ffn_down": nrm(ks[14], (N_DENSE, D_FF, D), BETA * D_FF ** -0.5),
        "router": nrm(ks[15], (N_MOE, D, N_EXPERTS), D ** -0.5),
        "exp_gate": nrm(ks[16], (N_MOE, N_EXPERTS, D, D_FF), D ** -0.5),
        "exp_up": nrm(ks[17], (N_MOE, N_EXPERTS, D, D_FF), D ** -0.5),
        "exp_down": nrm(ks[18], (N_MOE, N_EXPERTS, D_FF, D), BETA * D_FF ** -0.5),
        "ln2_g": 1.0 + nrm(ks[19], (DEPTH, D), 0.02),
        "ln2_b": nrm(ks[20], (DEPTH, D), 0.02),
    }


def reference(x_prompt, x_sample, emb_ln_g, emb_ln_b, w_in, attn_sink, na_rpb, gnorm_a, gnorm_b, w_o,
              ln1_g, ln1_b, ffn_gate, ffn_up, ffn_down, router, exp_gate, exp_up, exp_down, ln2_g, ln2_b):
    y_prompt = _trunk(x_prompt, emb_ln_g, emb_ln_b, w_in, attn_sink, na_rpb, gnorm_a, gnorm_b, w_o, ln1_g, ln1_b,
                      ffn_gate, ffn_up, ffn_down, router, exp_gate, exp_up, exp_down, ln2_g, ln2_b)
    y_sample = _trunk(x_sample, emb_ln_g, emb_ln_b, w_in, attn_sink, na_rpb, gnorm_a, gnorm_b, w_o, ln1_g, ln1_b,
                      ffn_gate, ffn_up, ffn_down, router, exp_gate, exp_up, exp_down, ln2_g, ln2_b)
    return (y_prompt, y_sample)
```

```python
import functools

import numpy as np
import jax
import jax.numpy as jnp
from jax import lax
from jax.experimental import pallas as pl
from jax.experimental.pallas import tpu as pltpu

D_MODEL = 1024
HEAD_DIM = 64
H_A = 8
HKV_A = 2
GQA = H_A // HKV_A
H_B = 8
WINDOW = 128
WBLK = 128
ROPE_THETA = 10000.0
GRID_W = 64
NA_ROWS = 8
NA_COLS = 16
D_FF = 2816
N_EXPERTS = 8
TOP_K = 2
LN_EPS = 1e-5
NEG = -1e30
QA_W = H_A * HEAD_DIM
KVA_W = HKV_A * HEAD_DIM
B_W = H_B * HEAD_DIM
IN_W = QA_W + 2 * KVA_W + 3 * B_W
ROPE_W = QA_W + KVA_W

V7X_VMEM_LIMIT_BYTES = 56 * 1024 * 1024
LANES = 128
MXU_N = 256
ROW_TILE = 512
NA_CHUNK_ROWS = 8
COMBINE_TILE = 256

F32 = jnp.float32
BF16 = jnp.bfloat16


def _layernorm(y, g, b):
    mu = jnp.mean(y, axis=-1, keepdims=True)
    yc = y - mu
    var = jnp.mean(yc * yc, axis=-1, keepdims=True)
    return yc * lax.rsqrt(var + LN_EPS) * g + b


def _rmsnorm(o, g):
    ms = jnp.mean(o * o, axis=-1, keepdims=True)
    return o * lax.rsqrt(ms + LN_EPS) * g


def _dot_nt(a, b):
    return lax.dot_general(a, b, (((1,), (1,)), ((), ())), preferred_element_type=F32)


def _params(n_axes=1, vmem=None):
    return pltpu.CompilerParams(dimension_semantics=("arbitrary",) * n_axes, vmem_limit_bytes=vmem)


def _embln_kernel(xa_ref, xb_ref, g_ref, b_ref, o_ref, *, n_a):
    i = pl.program_id(0)

    @pl.when(i < n_a)
    def _():
        o_ref[...] = _layernorm(xa_ref[...], g_ref[...], b_ref[...])

    @pl.when(i >= n_a)
    def _():
        o_ref[...] = _layernorm(xb_ref[...], g_ref[...], b_ref[...])


def _embln(xa, xb, g, b):
    ta, tb = xa.shape[0], xb.shape[0]
    tm = ROW_TILE
    n_a, n_b = ta // tm, tb // tm
    return pl.pallas_call(
        functools.partial(_embln_kernel, n_a=n_a),
        grid=(n_a + n_b,),
        in_specs=[
            pl.BlockSpec((tm, D_MODEL), lambda i: (jnp.minimum(i, n_a - 1), 0)),
            pl.BlockSpec((tm, D_MODEL), lambda i: (jnp.maximum(i - n_a, 0), 0)),
            pl.BlockSpec((1, D_MODEL), lambda i: (0, 0)),
            pl.BlockSpec((1, D_MODEL), lambda i: (0, 0)),
        ],
        out_specs=pl.BlockSpec((tm, D_MODEL), lambda i: (i, 0)),
        out_shape=jax.ShapeDtypeStruct((ta + tb, D_MODEL), F32),
        compiler_params=_params(),
        name="emb_ln",
    )(xa, xb, g, b)


def _inproj_kernel(pos_ref, x_ref, w_ref, cos_ref, sin_ref,
                   qa_ref, ka_ref, va_ref, qb_ref, kb_ref, vb_ref):
    del pos_ref
    xb = x_ref[...].astype(BF16)
    cos = cos_ref[...]
    sin = sin_ref[...]
    lane = lax.broadcasted_iota(jnp.int32, cos.shape, 1)
    first_half = (lane % HEAD_DIM) < (HEAD_DIM // 2)
    scale = HEAD_DIM ** -0.5

    def rope(h):
        partner = jnp.where(first_half, pltpu.roll(h, LANES - HEAD_DIM // 2, 1),
                            pltpu.roll(h, HEAD_DIM // 2, 1))
        return h * cos + partner * sin

    for c in range(ROPE_W // LANES):
        h = jnp.dot(xb, w_ref[:, c * LANES:(c + 1) * LANES], preferred_element_type=F32)
        r = rope(h)
        if c < QA_W // LANES:
            qa_ref[:, c * LANES:(c + 1) * LANES] = (r * scale).astype(BF16)
        else:
            ka_ref[...] = r.astype(BF16)
    off = ROPE_W
    va_ref[...] = jnp.dot(xb, w_ref[:, off:off + KVA_W], preferred_element_type=F32).astype(BF16)
    off += KVA_W
    for j, (ref, s) in enumerate(((qb_ref, scale), (kb_ref, 1.0), (vb_ref, 1.0))):
        for c in range(B_W // MXU_N):
            lo = off + j * B_W + c * MXU_N
            h = jnp.dot(xb, w_ref[:, lo:lo + MXU_N], preferred_element_type=F32)
            ref[:, c * MXU_N:(c + 1) * MXU_N] = (h * s).astype(BF16)


def _inproj(x, w_bf16, cos_t, sin_t, pos_blk):
    t = x.shape[0]
    tm = ROW_TILE
    row = lambda i, p: (i, 0)
    out_shapes = [jax.ShapeDtypeStruct((t, w), BF16) for w in (QA_W, KVA_W, KVA_W, B_W, B_W, B_W)]
    return pl.pallas_call(
        _inproj_kernel,
        grid_spec=pltpu.PrefetchScalarGridSpec(
            num_scalar_prefetch=1,
            grid=(t // tm,),
            in_specs=[
                pl.BlockSpec((tm, D_MODEL), row),
                pl.BlockSpec((D_MODEL, IN_W), lambda i, p: (0, 0)),
                pl.BlockSpec((tm, LANES), lambda i, p: (p[i], 0)),
                pl.BlockSpec((tm, LANES), lambda i, p: (p[i], 0)),
            ],
            out_specs=[pl.BlockSpec((tm, s.shape[1]), row) for s in out_shapes],
        ),
        out_shape=out_shapes,
        compiler_params=_params(vmem=V7X_VMEM_LIMIT_BYTES),
        name="in_proj_rope",
    )(pos_blk, x, w_bf16, cos_t, sin_t)


def _win_kernel(prev_ref, next_ref, btype_ref, sink_ref, q_ref, kp_ref, kc_ref, kn_ref,
                vp_ref, vc_ref, vn_ref, bias_ref, g_ref, o_ref):
    del prev_ref, next_ref, btype_ref
    q = q_ref[...]
    k3 = jnp.concatenate([kp_ref[...], kc_ref[...], kn_ref[...]], axis=0)
    v3 = jnp.concatenate([vp_ref[...], vc_ref[...], vn_ref[...]], axis=0)
    bias = bias_ref[0]
    outs = []
    for h in range(H_A):
        kv = h // GQA
        kh = k3[:, kv * HEAD_DIM:(kv + 1) * HEAD_DIM]
        vh = v3[:, kv * HEAD_DIM:(kv + 1) * HEAD_DIM]
        s = _dot_nt(q[:, h * HEAD_DIM:(h + 1) * HEAD_DIM], kh) + bias
        sk = sink_ref[h]
        m = jnp.maximum(jnp.max(s, axis=-1, keepdims=True), sk)
        p = jnp.exp(s - m)
        den = jnp.sum(p, axis=-1, keepdims=True) + jnp.exp(sk - m)
        o = jnp.dot(p.astype(BF16), vh, preferred_element_type=F32)
        outs.append(o / den)
    o = jnp.concatenate(outs, axis=-1)
    o_ref[...] = _rmsnorm(o, g_ref[...]).astype(BF16)


def _window_attn(qa, ka, va, sink, gnorm, tables):
    t = qa.shape[0]
    prev_blk, next_blk, btype, wbias = tables
    cur = pl.BlockSpec((WBLK, KVA_W), lambda i, pv, nx, bt: (i, 0))
    prv = pl.BlockSpec((WBLK, KVA_W), lambda i, pv, nx, bt: (pv[i], 0))
    nxt = pl.BlockSpec((WBLK, KVA_W), lambda i, pv, nx, bt: (nx[i], 0))
    return pl.pallas_call(
        _win_kernel,
        grid_spec=pltpu.PrefetchScalarGridSpec(
            num_scalar_prefetch=3,
            grid=(t // WBLK,),
            in_specs=[
                pl.BlockSpec(memory_space=pltpu.SMEM),
                pl.BlockSpec((WBLK, QA_W), lambda i, pv, nx, bt: (i, 0)),
                prv, cur, nxt, prv, cur, nxt,
                pl.BlockSpec((1, WBLK, 3 * WBLK), lambda i, pv, nx, bt: (bt[i], 0, 0)),
                pl.BlockSpec((1, QA_W), lambda i, pv, nx, bt: (0, 0)),
            ],
            out_specs=pl.BlockSpec((WBLK, QA_W), lambda i, pv, nx, bt: (i, 0)),
        ),
        out_shape=jax.ShapeDtypeStruct((t, QA_W), BF16),
        compiler_params=_params(),
        name="window_attn",
    )(prev_blk, next_blk, btype, sink, qa, ka, ka, ka, va, va, va, wbias, gnorm)


def _nbr_kernel(prev_ref, next_ref, row0_ref, rows_ref, q_ref, kp_ref, kc_ref, kn_ref,
                vp_ref, vc_ref, vn_ref, bias_ref, g_ref, o_ref, kwin, vwin):
    del prev_ref, next_ref
    j = pl.program_id(0)
    chunk = NA_CHUNK_ROWS * GRID_W
    for c, (kr, vr) in enumerate(((kp_ref, vp_ref), (kc_ref, vc_ref), (kn_ref, vn_ref))):
        kwin[c * chunk:(c + 1) * chunk, :] = kr[...]
        vwin[c * chunk:(c + 1) * chunk, :] = vr[...]
    row0 = row0_ref[j]
    rows = rows_ref[j]
    g = g_ref[...]
    nkeys = NA_ROWS * GRID_W

    def body(rr, carry):
        r = row0 + rr
        rstart = jnp.clip(r - NA_ROWS // 2, 0, rows - NA_ROWS)
        delta = r - rstart
        loc = pl.multiple_of((rstart - row0 + NA_CHUNK_ROWS) * GRID_W, GRID_W)
        kw = kwin[pl.ds(loc, nkeys), :]
        vw = vwin[pl.ds(loc, nkeys), :]
        q = q_ref[pl.ds(pl.multiple_of(rr * GRID_W, GRID_W), GRID_W), :]
        outs = []
        for h in range(H_B):
            sl = slice(h * HEAD_DIM, (h + 1) * HEAD_DIM)
            s = _dot_nt(q[:, sl], kw[:, sl])
            b = jnp.concatenate(
                [bias_ref[h, (NA_ROWS - 1) - delta + 2 * jj] for jj in range(NA_ROWS // 2)], axis=-1)
            s = s + b
            m = jnp.max(s, axis=-1, keepdims=True)
            p = jnp.exp(s - m)
            den = jnp.sum(p, axis=-1, keepdims=True)
            o = jnp.dot(p.astype(BF16), vw[:, sl], preferred_element_type=F32)
            outs.append(o / den)
        o = jnp.concatenate(outs, axis=-1)
        o_ref[pl.ds(pl.multiple_of(rr * GRID_W, GRID_W), GRID_W), :] = _rmsnorm(o, g).astype(BF16)
        return carry

    lax.fori_loop(0, NA_CHUNK_ROWS, body, 0)


def _nbr_attn(qb, kb, vb, bias_pairs, gnorm, tables):
    t = qb.shape[0]
    prev_c, next_c, row0, rows = tables
    chunk = NA_CHUNK_ROWS * GRID_W
    cur = pl.BlockSpec((chunk, B_W), lambda i, pv, nx, r0, rs: (i, 0))
    prv = pl.BlockSpec((chunk, B_W), lambda i, pv, nx, r0, rs: (pv[i], 0))
    nxt = pl.BlockSpec((chunk, B_W), lambda i, pv, nx, r0, rs: (nx[i], 0))
    return pl.pallas_call(
        _nbr_kernel,
        grid_spec=pltpu.PrefetchScalarGridSpec(
            num_scalar_prefetch=4,
            grid=(t // chunk,),
            in_specs=[
                cur, prv, cur, nxt, prv, cur, nxt,
                pl.BlockSpec(bias_pairs.shape, lambda i, pv, nx, r0, rs: (0, 0, 0, 0)),
                pl.BlockSpec((1, B_W), lambda i, pv, nx, r0, rs: (0, 0)),
            ],
            out_specs=cur,
            scratch_shapes=[pltpu.VMEM((3 * chunk, B_W), BF16), pltpu.VMEM((3 * chunk, B_W), BF16)],
        ),
        out_shape=jax.ShapeDtypeStruct((t, B_W), BF16),
        compiler_params=_params(vmem=V7X_VMEM_LIMIT_BYTES),
        name="nbr_attn",
    )(prev_c, next_c, row0, rows, qb, kb, kb, kb, vb, vb, vb, bias_pairs, gnorm)


def _outproj_kernel(x_ref, oa_ref, ob_ref, wa_ref, wb_ref, g_ref, b_ref, o_ref, *, alpha):
    mix = jnp.dot(oa_ref[...], wa_ref[...], preferred_element_type=F32)
    mix = mix + jnp.dot(ob_ref[...], wb_ref[...], preferred_element_type=F32)
    o_ref[...] = _layernorm(alpha * x_ref[...] + mix, g_ref[...], b_ref[...])


def _outproj(x, oa, ob, wa, wb, g, b, alpha):
    t = x.shape[0]
    tm = ROW_TILE
    row = lambda i: (i, 0)
    const = lambda i: (0, 0)
    return pl.pallas_call(
        functools.partial(_outproj_kernel, alpha=alpha),
        grid=(t // tm,),
        in_specs=[
            pl.BlockSpec((tm, D_MODEL), row),
            pl.BlockSpec((tm, QA_W), row),
            pl.BlockSpec((tm, B_W), row),
            pl.BlockSpec((QA_W, D_MODEL), const),
            pl.BlockSpec((B_W, D_MODEL), const),
            pl.BlockSpec((1, D_MODEL), const),
            pl.BlockSpec((1, D_MODEL), const),
        ],
        out_specs=pl.BlockSpec((tm, D_MODEL), row),
        out_shape=jax.ShapeDtypeStruct((t, D_MODEL), F32),
        compiler_params=_params(vmem=V7X_VMEM_LIMIT_BYTES),
        name="out_proj_ln1",
    )(x, oa, ob, wa, wb, g, b)


def _swiglu_tile(xb, wg_ref, wu_ref, wd_ref, h_ref):
    for c in range(D_FF // MXU_N):
        sl = slice(c * MXU_N, (c + 1) * MXU_N)
        gate = jnp.dot(xb, wg_ref[0, :, sl], preferred_element_type=F32)
        up = jnp.dot(xb, wu_ref[0, :, sl], preferred_element_type=F32)
        h_ref[:, sl] = (gate / (1.0 + jnp.exp(-gate)) * up).astype(BF16)
    return jnp.dot(h_ref[...], wd_ref[0], preferred_element_type=F32)


def _ffn_kernel(x_ref, wg_ref, wu_ref, wd_ref, g_ref, b_ref, o_ref, h_ref, *, alpha):
    x = x_ref[...]
    f = _swiglu_tile(x.astype(BF16), wg_ref, wu_ref, wd_ref, h_ref)
    o_ref[...] = _layernorm(alpha * x + f, g_ref[...], b_ref[...])


def _ffn_dense(x, wg, wu, wd, g, b, alpha):
    t = x.shape[0]
    tm = ROW_TILE
    row = lambda i: (i, 0)
    const = lambda i: (0, 0)
    const3 = lambda i: (0, 0, 0)
    return pl.pallas_call(
        functools.partial(_ffn_kernel, alpha=alpha),
        grid=(t // tm,),
        in_specs=[
            pl.BlockSpec((tm, D_MODEL), row),
            pl.BlockSpec((1, D_MODEL, D_FF), const3),
            pl.BlockSpec((1, D_MODEL, D_FF), const3),
            pl.BlockSpec((1, D_FF, D_MODEL), const3),
            pl.BlockSpec((1, D_MODEL), const),
            pl.BlockSpec((1, D_MODEL), const),
        ],
        out_specs=pl.BlockSpec((tm, D_MODEL), row),
        out_shape=jax.ShapeDtypeStruct((t, D_MODEL), F32),
        scratch_shapes=[pltpu.VMEM((tm, D_FF), BF16)],
        compiler_params=_params(vmem=V7X_VMEM_LIMIT_BYTES),
        name="ffn_dense_ln2",
    )(x, wg, wu, wd, g, b)


def _router_kernel(x_ref, rt_ref, idx_ref, w_ref):
    logits = lax.dot_general(rt_ref[...], x_ref[...], (((1,), (1,)), ((), ())),
                             precision=lax.Precision.HIGHEST, preferred_element_type=F32)
    e = lax.broadcasted_iota(jnp.int32, logits.shape, 0)
    m1 = jnp.max(logits, axis=0, keepdims=True)
    i1 = jnp.min(jnp.where(logits == m1, e, N_EXPERTS), axis=0, keepdims=True)
    rest = jnp.where(e == i1, -jnp.inf, logits)
    m2 = jnp.max(rest, axis=0, keepdims=True)
    i2 = jnp.min(jnp.where(rest == m2, e, N_EXPERTS), axis=0, keepdims=True)
    t2 = jnp.exp(m2 - m1)
    idx_ref[...] = jnp.concatenate([i1, i2], axis=0)
    w_ref[...] = jnp.concatenate([1.0 / (1.0 + t2), t2 / (1.0 + t2)], axis=0)


def _router(x, router_t):
    t = x.shape[0]
    tm = ROW_TILE
    return pl.pallas_call(
        _router_kernel,
        grid=(t // tm,),
        in_specs=[
            pl.BlockSpec((tm, D_MODEL), lambda i: (i, 0)),
            pl.BlockSpec((N_EXPERTS, D_MODEL), lambda i: (0, 0)),
        ],
        out_specs=[pl.BlockSpec((TOP_K, tm), lambda i: (0, i)), pl.BlockSpec((TOP_K, tm), lambda i: (0, i))],
        out_shape=[jax.ShapeDtypeStruct((TOP_K, t), jnp.int32), jax.ShapeDtypeStruct((TOP_K, t), F32)],
        compiler_params=_params(),
        name="router_top2",
    )(x, router_t)


def _gather_rows(idx_ref, n, src_hbm, dst, sem):
    def body(r, carry):
        pltpu.make_async_copy(src_hbm.at[pl.ds(idx_ref[0, 0, r], 1), :], dst.at[pl.ds(r, 1), :], sem).start()
        return carry
    lax.fori_loop(0, n, body, 0, unroll=8)


def _wait_rows(src_hbm, dst, sem):
    pltpu.make_async_copy(src_hbm.at[pl.ds(0, dst.shape[0]), :], dst, sem).wait()


def _moe_kernel(tile_e_ref, nvalid_ref, idx_cur_ref, idx_next_ref, x_hbm, wg_ref, wu_ref, wd_ref,
                y_ref, xbuf, sem, h_ref):
    del tile_e_ref
    j = pl.program_id(0)
    nv = nvalid_ref[0]
    tm = xbuf.shape[1]
    slot = j % 2

    @pl.when(j == 0)
    def _():
        _gather_rows(idx_cur_ref, tm, x_hbm, xbuf.at[0], sem.at[0])

    @pl.when(j + 1 < nv)
    def _():
        _gather_rows(idx_next_ref, tm, x_hbm, xbuf.at[1 - slot], sem.at[1 - slot])

    @pl.when(j < nv)
    def _():
        _wait_rows(x_hbm, xbuf.at[slot], sem.at[slot])
        y_ref[...] = _swiglu_tile(xbuf[slot].astype(BF16), wg_ref, wu_ref, wd_ref, h_ref)

    @pl.when(j >= nv)
    def _():
        y_ref[...] = jnp.zeros_like(y_ref)


def _moe_experts(x, tok_sorted, tile_e, nvalid, wg, wu, wd):
    tm = ROW_TILE
    n_tiles = tok_sorted.shape[0]
    wspec = lambda shape: pl.BlockSpec((1,) + shape, lambda j, te, nv: (te[j], 0, 0))
    idx_shape = (1, 1, tm)
    return pl.pallas_call(
        _moe_kernel,
        grid_spec=pltpu.PrefetchScalarGridSpec(
            num_scalar_prefetch=2,
            grid=(n_tiles,),
            in_specs=[
                pl.BlockSpec(idx_shape, lambda j, te, nv: (j, 0, 0), memory_space=pltpu.SMEM),
                pl.BlockSpec(idx_shape, lambda j, te, nv: (jnp.minimum(j + 1, n_tiles - 1), 0, 0),
                             memory_space=pltpu.SMEM),
                pl.BlockSpec(memory_space=pl.ANY),
                wspec((D_MODEL, D_FF)), wspec((D_MODEL, D_FF)), wspec((D_FF, D_MODEL)),
            ],
            out_specs=pl.BlockSpec((tm, D_MODEL), lambda j, te, nv: (j, 0)),
            scratch_shapes=[
                pltpu.VMEM((2, tm, D_MODEL), F32),
                pltpu.SemaphoreType.DMA((2,)),
                pltpu.VMEM((tm, D_FF), BF16),
            ],
        ),
        out_shape=jax.ShapeDtypeStruct((n_tiles * tm, D_MODEL), F32),
        compiler_params=_params(vmem=V7X_VMEM_LIMIT_BYTES),
        name="moe_experts",
    )(tile_e, nvalid, tok_sorted, tok_sorted, x, wg, wu, wd)


def _combine_kernel(pos_cur_ref, pos_next_ref, x_ref, w_ref, y_hbm, g_ref, b_ref, o_ref, ybuf, sem, *, alpha):
    j = pl.program_id(0)
    n = pl.num_programs(0)
    slot = j % 2
    rows = ybuf.shape[1]

    @pl.when(j == 0)
    def _():
        _gather_rows(pos_cur_ref, rows, y_hbm, ybuf.at[0], sem.at[0])

    @pl.when(j + 1 < n)
    def _():
        _gather_rows(pos_next_ref, rows, y_hbm, ybuf.at[1 - slot], sem.at[1 - slot])

    _wait_rows(y_hbm, ybuf.at[slot], sem.at[slot])
    tc = rows // TOP_K
    w = w_ref[...]
    f = w[:, 0:1] * ybuf[slot, 0:tc, :] + w[:, 1:2] * ybuf[slot, tc:2 * tc, :]
    o_ref[...] = _layernorm(alpha * x_ref[...] + f, g_ref[...], b_ref[...])


def _moe_combine(x, gate_w, pos_tiles, y_sorted, g, b, alpha):
    t = x.shape[0]
    tc = COMBINE_TILE
    n = t // tc
    pos_shape = (1, 1, TOP_K * tc)
    return pl.pallas_call(
        functools.partial(_combine_kernel, alpha=alpha),
        grid=(n,),
        in_specs=[
            pl.BlockSpec(pos_shape, lambda j: (j, 0, 0), memory_space=pltpu.SMEM),
            pl.BlockSpec(pos_shape, lambda j: (jnp.minimum(j + 1, n - 1), 0, 0), memory_space=pltpu.SMEM),
            pl.BlockSpec((tc, D_MODEL), lambda j: (j, 0)),
            pl.BlockSpec((tc, TOP_K), lambda j: (j, 0)),
            pl.BlockSpec(memory_space=pl.ANY),
            pl.BlockSpec((1, D_MODEL), lambda j: (0, 0)),
            pl.BlockSpec((1, D_MODEL), lambda j: (0, 0)),
        ],
        out_specs=pl.BlockSpec((tc, D_MODEL), lambda j: (j, 0)),
        out_shape=jax.ShapeDtypeStruct((t, D_MODEL), F32),
        scratch_shapes=[pltpu.VMEM((2, TOP_K * tc, D_MODEL), F32), pltpu.SemaphoreType.DMA((2,))],
        compiler_params=_params(),
        name="moe_combine_ln2",
    )(pos_tiles, pos_tiles, x, gate_w, y_sorted, g, b)


def _route(idx_t, tm, tc):
    t = idx_t.shape[0]
    a = t * TOP_K
    e_flat = idx_t.reshape(a)
    onehot = (e_flat[:, None] == jnp.arange(N_EXPERTS, dtype=jnp.int32)[None, :]).astype(jnp.int32)
    csum = jnp.cumsum(onehot, axis=0)
    rank = jnp.sum(onehot * csum, axis=1) - 1
    counts = csum[-1]
    padded = ((counts + tm - 1) // tm) * tm
    ends = jnp.cumsum(padded)
    pos = jnp.sum(onehot * (ends - padded)[None, :], axis=1) + rank
    n_tiles = a // tm + N_EXPERTS
    tok_sorted = jnp.zeros((n_tiles * tm,), jnp.int32).at[pos].set(jnp.arange(a, dtype=jnp.int32) // TOP_K)
    nvalid = (ends[-1] // tm).astype(jnp.int32)
    starts = jnp.arange(n_tiles, dtype=jnp.int32) * tm
    tile_e = jnp.minimum(jnp.sum((starts[:, None] >= ends[None, :]).astype(jnp.int32), axis=1), N_EXPERTS - 1)
    tile_e = jnp.where(jnp.arange(n_tiles) < nvalid, tile_e, tile_e[nvalid - 1])
    pos_tiles = pos.reshape(t // tc, tc, TOP_K).transpose(0, 2, 1).reshape(t // tc, 1, TOP_K * tc)
    return tok_sorted.reshape(n_tiles, 1, tm), tile_e, nvalid.reshape(1), pos_tiles


def _moe_layer(x, router_t, wg, wu, wd, g, b, alpha):
    idx, gate_w = _router(x, router_t)
    tok_sorted, tile_e, nvalid, pos_tiles = _route(idx.T, ROW_TILE, COMBINE_TILE)
    y_sorted = _moe_experts(x, tok_sorted, tile_e, nvalid, wg, wu, wd)
    return _moe_combine(x, gate_w.T, pos_tiles, y_sorted, g, b, alpha)


def _segment_tables(groups, unit):
    prev, nxt, within, per_seq = [], [], [], []
    base = 0
    for n_seq, seq_len in groups:
        nb = seq_len // unit
        for _ in range(n_seq):
            for n in range(nb):
                i = base + n
                prev.append(i - 1 if n > 0 else i)
                nxt.append(i + 1 if n < nb - 1 else i)
                within.append(n)
                per_seq.append(nb)
            base += nb
    as_i32 = lambda v: jnp.asarray(np.asarray(v, np.int32))
    return as_i32(prev), as_i32(nxt), as_i32(within), as_i32(per_seq)


def _window_tables(groups):
    prev, nxt, within, per_seq = _segment_tables(groups, WBLK)
    btype = jnp.where(within == 0, 0, jnp.where(within == per_seq - 1, 2, 1)).astype(jnp.int32)
    i = np.arange(WBLK)[:, None]
    jj = np.arange(3 * WBLK)[None, :]
    band = np.abs(jj - WBLK - i) <= WINDOW
    masks = []
    for first, last in ((True, False), (False, False), (False, True)):
        inr = np.ones_like(band)
        if first:
            inr = inr & (jj >= WBLK)
        if last:
            inr = inr & (jj < 2 * WBLK)
        masks.append(np.where(band & inr, 0.0, NEG))
    return prev, nxt, btype, jnp.asarray(np.stack(masks).astype(np.float32))


def _nbr_tables(groups):
    chunk = NA_CHUNK_ROWS * GRID_W
    prev, nxt, within, per_seq = _segment_tables(groups, chunk)
    return prev, nxt, within * NA_CHUNK_ROWS, per_seq * NA_CHUNK_ROWS


def _nbr_bias_pairs(na_rpb):
    c = np.arange(GRID_W)[:, None]
    kc = np.arange(GRID_W)[None, :]
    cstart = np.clip(c - NA_COLS // 2, 0, GRID_W - NA_COLS)
    allowed = (kc >= cstart) & (kc < cstart + NA_COLS)
    dc = np.clip(kc - c + (NA_COLS - 1), 0, 2 * NA_COLS - 2)
    full = jnp.where(jnp.asarray(allowed)[None, None, None], na_rpb.astype(F32)[:, :, :, dc], NEG)
    return jnp.concatenate([full[:, :, :-1], full[:, :, 1:]], axis=-1)


def _rope_tables(max_len):
    half = HEAD_DIM // 2
    inv = 1.0 / (ROPE_THETA ** (jnp.arange(half, dtype=F32) / half))
    ang = jnp.arange(max_len, dtype=F32)[:, None] * inv[None, :]
    cos, sin = jnp.cos(ang), jnp.sin(ang)
    reps = LANES // HEAD_DIM
    return jnp.tile(jnp.concatenate([cos, cos], axis=-1), (1, reps)), \
        jnp.tile(jnp.concatenate([-sin, sin], axis=-1), (1, reps))


def _trunk(xa, xb, groups, emb_ln_g, emb_ln_b, w_in, attn_sink, na_rpb, gnorm_a, gnorm_b, w_o, ln1_g, ln1_b,
           ffn_gate, ffn_up, ffn_down, router, exp_gate, exp_up, exp_down, ln2_g, ln2_b):
    depth = w_in.shape[0]
    alpha = (2.0 * depth) ** 0.25
    for n_seq, seq_len in groups:
        assert seq_len % (NA_CHUNK_ROWS * GRID_W) == 0 and seq_len // WBLK >= 2
        assert (n_seq * seq_len) % ROW_TILE == 0 and seq_len % ROW_TILE == 0
    row2 = lambda v: v.reshape(1, -1)

    win_tables = _window_tables(groups)
    nbr_tables = _nbr_tables(groups)
    cos_t, sin_t = _rope_tables(max(s for _, s in groups))
    _, _, pos_within, _ = _segment_tables(groups, ROW_TILE)
    bias_pairs = _nbr_bias_pairs(na_rpb)
    w_in_b = w_in.astype(BF16)
    w_o_b = w_o.astype(BF16)
    ffn_b = [w.astype(BF16) for w in (ffn_gate, ffn_up, ffn_down)]
    exp_b = [w.astype(BF16) for w in (exp_gate, exp_up, exp_down)]
    router_t = jnp.swapaxes(router, 1, 2)

    x = _embln(xa, xb, row2(emb_ln_g), row2(emb_ln_b))
    for l in range(depth):
        qa, ka, va, qb, kb, vb = _inproj(x, w_in_b[l], cos_t, sin_t, pos_within)
        oa = _window_attn(qa, ka, va, attn_sink[l], row2(gnorm_a[l]), win_tables)
        ob = _nbr_attn(qb, kb, vb, bias_pairs[l], row2(gnorm_b[l]), nbr_tables)
        x = _outproj(x, oa, ob, w_o_b[l, :QA_W], w_o_b[l, QA_W:], row2(ln1_g[l]), row2(ln1_b[l]), alpha)
        i = l // 2
        if l % 2 == 0:
            x = _ffn_dense(x, ffn_b[0][i:i + 1], ffn_b[1][i:i + 1], ffn_b[2][i:i + 1],
                           row2(ln2_g[l]), row2(ln2_b[l]), alpha)
        else:
            x = _moe_layer(x, router_t[i], exp_b[0][i], exp_b[1][i], exp_b[2][i],
                           row2(ln2_g[l]), row2(ln2_b[l]), alpha)
    return x


def kernel(x_prompt, x_sample, emb_ln_g, emb_ln_b, w_in, attn_sink, na_rpb, gnorm_a, gnorm_b, w_o, ln1_g, ln1_b,
           ffn_gate, ffn_up, ffn_down, router, exp_gate, exp_up, exp_down, ln2_g, ln2_b):
    groups = (x_prompt.shape[:2], x_sample.shape[:2])
    ta = x_prompt.shape[0] * x_prompt.shape[1]
    y = _trunk(x_prompt.reshape(ta, D_MODEL), x_sample.reshape(-1, D_MODEL), groups,
               emb_ln_g, emb_ln_b, w_in, attn_sink, na_rpb, gnorm_a, gnorm_b, w_o, ln1_g, ln1_b,
               ffn_gate, ffn_up, ffn_down, router, exp_gate, exp_up, exp_down, ln2_g, ln2_b)
    return y[:ta].reshape(x_prompt.shape), y[ta:].reshape(x_sample.shape)
```

```python
import functools

import numpy as np
import jax
import jax.numpy as jnp
from jax import lax
from jax.experimental import pallas as pl
from jax.experimental.pallas import tpu as pltpu

D_MODEL = 1024
HEAD_DIM = 64
H_A = 8
HKV_A = 2
GQA = H_A // HKV_A
H_B = 8
WINDOW = 128
WBLK = 128
ROPE_THETA = 10000.0
GRID_W = 64
NA_ROWS = 8
NA_COLS = 16
D_FF = 2816
N_EXPERTS = 8
TOP_K = 2
LN_EPS = 1e-5
NEG = -1e30
QA_W = H_A * HEAD_DIM
KVA_W = HKV_A * HEAD_DIM
B_W = H_B * HEAD_DIM
IN_W = QA_W + 2 * KVA_W + 3 * B_W
ROPE_W = QA_W + KVA_W

V7X_VMEM_LIMIT_BYTES = 56 * 1024 * 1024
LANES = 128
MXU_N = 256
ROW_TILE = 512
NA_CHUNK_ROWS = 8
COMBINE_TILE = 256

F32 = jnp.float32
BF16 = jnp.bfloat16


def _layernorm(y, g, b):
    mu = jnp.mean(y, axis=-1, keepdims=True)
    yc = y - mu
    var = jnp.mean(yc * yc, axis=-1, keepdims=True)
    return yc * lax.rsqrt(var + LN_EPS) * g + b


def _rmsnorm(o, g):
    ms = jnp.mean(o * o, axis=-1, keepdims=True)
    return o * lax.rsqrt(ms + LN_EPS) * g


def _dot_nt(a, b):
    return lax.dot_general(a, b, (((1,), (1,)), ((), ())), preferred_element_type=F32)


def _params(n_axes=1, vmem=None):
    return pltpu.CompilerParams(dimension_semantics=("arbitrary",) * n_axes, vmem_limit_bytes=vmem)


def _embln_kernel(xa_ref, xb_ref, g_ref, b_ref, o_ref, *, n_a):
    i = pl.program_id(0)

    @pl.when(i < n_a)
    def _():
        o_ref[...] = _layernorm(xa_ref[...], g_ref[...], b_ref[...])

    @pl.when(i >= n_a)
    def _():
        o_ref[...] = _layernorm(xb_ref[...], g_ref[...], b_ref[...])


def _embln(xa, xb, g, b):
    ta, tb = xa.shape[0], xb.shape[0]
    tm = ROW_TILE
    n_a, n_b = ta // tm, tb // tm
    return pl.pallas_call(
        functools.partial(_embln_kernel, n_a=n_a),
        grid=(n_a + n_b,),
        in_specs=[
            pl.BlockSpec((tm, D_MODEL), lambda i: (jnp.minimum(i, n_a - 1), 0)),
            pl.BlockSpec((tm, D_MODEL), lambda i: (jnp.maximum(i - n_a, 0), 0)),
            pl.BlockSpec((1, D_MODEL), lambda i: (0, 0)),
            pl.BlockSpec((1, D_MODEL), lambda i: (0, 0)),
        ],
        out_specs=pl.BlockSpec((tm, D_MODEL), lambda i: (i, 0)),
        out_shape=jax.ShapeDtypeStruct((ta + tb, D_MODEL), F32),
        compiler_params=_params(),
        name="emb_ln",
    )(xa, xb, g, b)


def _inproj_kernel(pos_ref, x_ref, w_ref, cos_ref, sin_ref,
                   qa_ref, ka_ref, va_ref, qb_ref, kb_ref, vb_ref):
    del pos_ref
    xb = x_ref[...].astype(BF16)
    cos = cos_ref[...]
    sin = sin_ref[...]
    lane = lax.broadcasted_iota(jnp.int32, cos.shape, 1)
    first_half = (lane % HEAD_DIM) < (HEAD_DIM // 2)
    scale = HEAD_DIM ** -0.5

    def rope(h):
        partner = jnp.where(first_half, pltpu.roll(h, LANES - HEAD_DIM // 2, 1),
                            pltpu.roll(h, HEAD_DIM // 2, 1))
        return h * cos + partner * sin

    for c in range(ROPE_W // LANES):
        h = jnp.dot(xb, w_ref[:, c * LANES:(c + 1) * LANES], preferred_element_type=F32)
        r = rope(h)
        if c < QA_W // LANES:
            qa_ref[:, c * LANES:(c + 1) * LANES] = (r * scale).astype(BF16)
        else:
            ka_ref[...] = r.astype(BF16)
    off = ROPE_W
    va_ref[...] = jnp.dot(xb, w_ref[:, off:off + KVA_W], preferred_element_type=F32).astype(BF16)
    off += KVA_W
    for j, (ref, s) in enumerate(((qb_ref, scale), (kb_ref, 1.0), (vb_ref, 1.0))):
        for c in range(B_W // MXU_N):
            lo = off + j * B_W + c * MXU_N
            h = jnp.dot(xb, w_ref[:, lo:lo + MXU_N], preferred_element_type=F32)
            ref[:, c * MXU_N:(c + 1) * MXU_N] = (h * s).astype(BF16)


def _inproj(x, w_bf16, cos_t, sin_t, pos_blk):
    t = x.shape[0]
    tm = ROW_TILE
    row = lambda i, p: (i, 0)
    out_shapes = [jax.ShapeDtypeStruct((t, w), BF16) for w in (QA_W, KVA_W, KVA_W, B_W, B_W, B_W)]
    return pl.pallas_call(
        _inproj_kernel,
        grid_spec=pltpu.PrefetchScalarGridSpec(
            num_scalar_prefetch=1,
            grid=(t // tm,),
            in_specs=[
                pl.BlockSpec((tm, D_MODEL), row),
                pl.BlockSpec((D_MODEL, IN_W), lambda i, p: (0, 0)),
                pl.BlockSpec((tm, LANES), lambda i, p: (p[i], 0)),
                pl.BlockSpec((tm, LANES), lambda i, p: (p[i], 0)),
            ],
            out_specs=[pl.BlockSpec((tm, s.shape[1]), row) for s in out_shapes],
        ),
        out_shape=out_shapes,
        compiler_params=_params(vmem=V7X_VMEM_LIMIT_BYTES),
        name="in_proj_rope",
    )(pos_blk, x, w_bf16, cos_t, sin_t)


def _split_heads(qc):
    lo = lax.broadcasted_iota(jnp.int32, qc.shape, 1) < HEAD_DIM
    zero = jnp.zeros_like(qc)
    return jnp.concatenate([jnp.where(lo, qc, zero), jnp.where(lo, zero, qc)], axis=0)


def _merge_heads(o2):
    m = o2.shape[0] // 2
    lo = lax.broadcasted_iota(jnp.int32, (m, LANES), 1) < HEAD_DIM
    return jnp.where(lo, o2[:m], o2[m:])


def _win_kernel(prev_ref, next_ref, first_ref, last_ref, sink_ref, q_ref, kp_ref, kc_ref, kn_ref,
                vp_ref, vc_ref, vn_ref, bias_ref, g_ref, o_ref, kwin, vwin):
    del prev_ref, next_ref
    i = pl.program_id(0)
    tq = q_ref.shape[0]
    nsub = tq // WBLK
    kwin[0:WBLK, :] = kp_ref[...]
    kwin[WBLK:WBLK + tq, :] = kc_ref[...]
    kwin[WBLK + tq:, :] = kn_ref[...]
    vwin[0:WBLK, 0:KVA_W] = vp_ref[...]
    vwin[WBLK:WBLK + tq, 0:KVA_W] = vc_ref[...]
    vwin[WBLK + tq:, 0:KVA_W] = vn_ref[...]
    vwin[:, KVA_W:] = jnp.ones((vwin.shape[0], LANES), BF16)
    first = first_ref[i]
    last = last_ref[i]
    g = g_ref[...]
    upper = lax.broadcasted_iota(jnp.int32, (2 * WBLK, 1), 0) < WBLK

    def body(j, carry):
        off = pl.multiple_of(j * WBLK, WBLK)
        bt = jnp.where((j == 0) & (first == 1), 0, jnp.where((j == nsub - 1) & (last == 1), 2, 1))
        bias = bias_ref[bt]
        bias2 = jnp.concatenate([bias, bias], axis=0)
        k3 = kwin[pl.ds(off, 3 * WBLK), :]
        v3 = vwin[pl.ds(off, 3 * WBLK), :]
        q = q_ref[pl.ds(off, WBLK), :]
        outs = []
        for c in range(QA_W // LANES):
            s = _dot_nt(_split_heads(q[:, c * LANES:(c + 1) * LANES]), k3) + bias2
            sk = jnp.where(upper, sink_ref[c], sink_ref[GQA + c])
            m = jnp.maximum(jnp.max(s, axis=-1, keepdims=True), sk)
            p = jnp.exp(s - m).astype(BF16)
            of = jnp.dot(p, v3, preferred_element_type=F32)
            den = of[:, LANES:] + jnp.exp(sk - m)
            outs.append(_merge_heads(of[:, :LANES] / den))
        o = jnp.concatenate(outs, axis=-1)
        o_ref[pl.ds(off, WBLK), :] = _rmsnorm(o, g).astype(BF16)
        return carry

    lax.fori_loop(0, nsub, body, 0, unroll=True)


def _window_attn(qa, ka, va, sink, gnorm, tables):
    t = qa.shape[0]
    tq = ROW_TILE
    prev_blk, next_blk, first, last, wbias = tables
    cur = pl.BlockSpec((tq, KVA_W), lambda i, pv, nx, fi, la: (i, 0))
    prv = pl.BlockSpec((WBLK, KVA_W), lambda i, pv, nx, fi, la: (pv[i], 0))
    nxt = pl.BlockSpec((WBLK, KVA_W), lambda i, pv, nx, fi, la: (nx[i], 0))
    return pl.pallas_call(
        _win_kernel,
        grid_spec=pltpu.PrefetchScalarGridSpec(
            num_scalar_prefetch=4,
            grid=(t // tq,),
            in_specs=[
                pl.BlockSpec(memory_space=pltpu.SMEM),
                pl.BlockSpec((tq, QA_W), lambda i, pv, nx, fi, la: (i, 0)),
                prv, cur, nxt, prv, cur, nxt,
                pl.BlockSpec(wbias.shape, lambda i, pv, nx, fi, la: (0, 0, 0)),
                pl.BlockSpec((1, QA_W), lambda i, pv, nx, fi, la: (0, 0)),
            ],
            out_specs=pl.BlockSpec((tq, QA_W), lambda i, pv, nx, fi, la: (i, 0)),
            scratch_shapes=[pltpu.VMEM((tq + 2 * WBLK, KVA_W), BF16),
                            pltpu.VMEM((tq + 2 * WBLK, KVA_W + LANES), BF16)],
        ),
        out_shape=jax.ShapeDtypeStruct((t, QA_W), BF16),
        compiler_params=_params(),
        name="window_attn",
    )(prev_blk, next_blk, first, last, sink, qa, ka, ka, ka, va, va, va, wbias, gnorm)


def _nbr_kernel(prev_ref, next_ref, row0_ref, rows_ref, q_ref, kp_ref, kc_ref, kn_ref,
                vp_ref, vc_ref, vn_ref, bias_ref, g_ref, o_ref, kwin, vwin):
    del prev_ref, next_ref
    j = pl.program_id(0)
    chunk = NA_CHUNK_ROWS * GRID_W
    ones = jnp.ones((3 * chunk, LANES), BF16)
    for c in range(B_W // LANES):
        vwin[:, (2 * c + 1) * LANES:(2 * c + 2) * LANES] = ones
    for w, (kr, vr) in enumerate(((kp_ref, vp_ref), (kc_ref, vc_ref), (kn_ref, vn_ref))):
        kwin[w * chunk:(w + 1) * chunk, :] = kr[...]
        for c in range(B_W // LANES):
            vwin[w * chunk:(w + 1) * chunk, 2 * c * LANES:(2 * c + 1) * LANES] = vr[:, c * LANES:(c + 1) * LANES]
    row0 = row0_ref[j]
    rows = rows_ref[j]
    g = g_ref[...]
    nkeys = NA_ROWS * GRID_W

    def body(rr, carry):
        r = row0 + rr
        rstart = jnp.clip(r - NA_ROWS // 2, 0, rows - NA_ROWS)
        bidx = (NA_ROWS - 1) - (r - rstart)
        loc = pl.multiple_of((rstart - row0 + NA_CHUNK_ROWS) * GRID_W, GRID_W)
        qoff = pl.multiple_of(rr * GRID_W, GRID_W)
        q = q_ref[pl.ds(qoff, GRID_W), :]
        outs = []
        for c in range(B_W // LANES):
            kc = kwin[pl.ds(loc, nkeys), c * LANES:(c + 1) * LANES]
            vc = vwin[pl.ds(loc, nkeys), 2 * c * LANES:(2 * c + 2) * LANES]
            s = _dot_nt(_split_heads(q[:, c * LANES:(c + 1) * LANES]), kc)
            s = s + jnp.concatenate([bias_ref[c, bidx + 2 * jj] for jj in range(NA_ROWS // 2)], axis=-1)
            m = jnp.max(s, axis=-1, keepdims=True)
            p = jnp.exp(s - m).astype(BF16)
            of = jnp.dot(p, vc, preferred_element_type=F32)
            outs.append(_merge_heads(of[:, :LANES] / of[:, LANES:]))
        o = jnp.concatenate(outs, axis=-1)
        o_ref[pl.ds(qoff, GRID_W), :] = _rmsnorm(o, g).astype(BF16)
        return carry

    lax.fori_loop(0, NA_CHUNK_ROWS, body, 0, unroll=2)


def _nbr_attn(qb, kb, vb, bias_pairs, gnorm, tables):
    t = qb.shape[0]
    prev_c, next_c, row0, rows = tables
    chunk = NA_CHUNK_ROWS * GRID_W
    cur = pl.BlockSpec((chunk, B_W), lambda i, pv, nx, r0, rs: (i, 0))
    prv = pl.BlockSpec((chunk, B_W), lambda i, pv, nx, r0, rs: (pv[i], 0))
    nxt = pl.BlockSpec((chunk, B_W), lambda i, pv, nx, r0, rs: (nx[i], 0))
    return pl.pallas_call(
        _nbr_kernel,
        grid_spec=pltpu.PrefetchScalarGridSpec(
            num_scalar_prefetch=4,
            grid=(t // chunk,),
            in_specs=[
                cur, prv, cur, nxt, prv, cur, nxt,
                pl.BlockSpec(bias_pairs.shape, lambda i, pv, nx, r0, rs: (0, 0, 0, 0)),
                pl.BlockSpec((1, B_W), lambda i, pv, nx, r0, rs: (0, 0)),
            ],
            out_specs=cur,
            scratch_shapes=[pltpu.VMEM((3 * chunk, B_W), BF16), pltpu.VMEM((3 * chunk, 2 * B_W), BF16)],
        ),
        out_shape=jax.ShapeDtypeStruct((t, B_W), BF16),
        compiler_params=_params(vmem=V7X_VMEM_LIMIT_BYTES),
        name="nbr_attn",
    )(prev_c, next_c, row0, rows, qb, kb, kb, kb, vb, vb, vb, bias_pairs, gnorm)


def _outproj_kernel(x_ref, oa_ref, ob_ref, wa_ref, wb_ref, g_ref, b_ref, o_ref, *, alpha):
    mix = jnp.dot(oa_ref[...], wa_ref[...], preferred_element_type=F32)
    mix = mix + jnp.dot(ob_ref[...], wb_ref[...], preferred_element_type=F32)
    o_ref[...] = _layernorm(alpha * x_ref[...] + mix, g_ref[...], b_ref[...])


def _outproj(x, oa, ob, wa, wb, g, b, alpha):
    t = x.shape[0]
    tm = ROW_TILE
    row = lambda i: (i, 0)
    const = lambda i: (0, 0)
    return pl.pallas_call(
        functools.partial(_outproj_kernel, alpha=alpha),
        grid=(t // tm,),
        in_specs=[
            pl.BlockSpec((tm, D_MODEL), row),
            pl.BlockSpec((tm, QA_W), row),
            pl.BlockSpec((tm, B_W), row),
            pl.BlockSpec((QA_W, D_MODEL), const),
            pl.BlockSpec((B_W, D_MODEL), const),
            pl.BlockSpec((1, D_MODEL), const),
            pl.BlockSpec((1, D_MODEL), const),
        ],
        out_specs=pl.BlockSpec((tm, D_MODEL), row),
        out_shape=jax.ShapeDtypeStruct((t, D_MODEL), F32),
        compiler_params=_params(vmem=V7X_VMEM_LIMIT_BYTES),
        name="out_proj_ln1",
    )(x, oa, ob, wa, wb, g, b)


def _swiglu_tile(xb, wg_ref, wu_ref, wd_ref, h_ref):
    for c in range(D_FF // MXU_N):
        sl = slice(c * MXU_N, (c + 1) * MXU_N)
        gate = jnp.dot(xb, wg_ref[0, :, sl], preferred_element_type=F32)
        up = jnp.dot(xb, wu_ref[0, :, sl], preferred_element_type=F32)
        h_ref[:, sl] = (gate / (1.0 + jnp.exp(-gate)) * up).astype(BF16)
    return jnp.dot(h_ref[...], wd_ref[0], preferred_element_type=F32)


def _ffn_kernel(x_ref, wg_ref, wu_ref, wd_ref, g_ref, b_ref, o_ref, h_ref, *, alpha):
    x = x_ref[...]
    f = _swiglu_tile(x.astype(BF16), wg_ref, wu_ref, wd_ref, h_ref)
    o_ref[...] = _layernorm(alpha * x + f, g_ref[...], b_ref[...])


def _ffn_dense(x, wg, wu, wd, g, b, alpha):
    t = x.shape[0]
    tm = ROW_TILE
    row = lambda i: (i, 0)
    const = lambda i: (0, 0)
    const3 = lambda i: (0, 0, 0)
    return pl.pallas_call(
        functools.partial(_ffn_kernel, alpha=alpha),
        grid=(t // tm,),
        in_specs=[
            pl.BlockSpec((tm, D_MODEL), row),
            pl.BlockSpec((1, D_MODEL, D_FF), const3),
            pl.BlockSpec((1, D_MODEL, D_FF), const3),
            pl.BlockSpec((1, D_FF, D_MODEL), const3),
            pl.BlockSpec((1, D_MODEL), const),
            pl.BlockSpec((1, D_MODEL), const),
        ],
        out_specs=pl.BlockSpec((tm, D_MODEL), row),
        out_shape=jax.ShapeDtypeStruct((t, D_MODEL), F32),
        scratch_shapes=[pltpu.VMEM((tm, D_FF), BF16)],
        compiler_params=_params(vmem=V7X_VMEM_LIMIT_BYTES),
        name="ffn_dense_ln2",
    )(x, wg, wu, wd, g, b)


def _router_kernel(x_ref, rt_ref, idx_ref, w_ref):
    logits = lax.dot_general(rt_ref[...], x_ref[...], (((1,), (1,)), ((), ())),
                             precision=lax.Precision.HIGHEST, preferred_element_type=F32)
    e = lax.broadcasted_iota(jnp.int32, logits.shape, 0)
    m1 = jnp.max(logits, axis=0, keepdims=True)
    i1 = jnp.min(jnp.where(logits == m1, e, N_EXPERTS), axis=0, keepdims=True)
    rest = jnp.where(e == i1, -jnp.inf, logits)
    m2 = jnp.max(rest, axis=0, keepdims=True)
    i2 = jnp.min(jnp.where(rest == m2, e, N_EXPERTS), axis=0, keepdims=True)
    t2 = jnp.exp(m2 - m1)
    idx_ref[...] = jnp.concatenate([i1, i2], axis=0)
    w_ref[...] = jnp.concatenate([1.0 / (1.0 + t2), t2 / (1.0 + t2)], axis=0)


def _router(x, router_t):
    t = x.shape[0]
    tm = ROW_TILE
    return pl.pallas_call(
        _router_kernel,
        grid=(t // tm,),
        in_specs=[
            pl.BlockSpec((tm, D_MODEL), lambda i: (i, 0)),
            pl.BlockSpec((N_EXPERTS, D_MODEL), lambda i: (0, 0)),
        ],
        out_specs=[pl.BlockSpec((TOP_K, tm), lambda i: (0, i)), pl.BlockSpec((TOP_K, tm), lambda i: (0, i))],
        out_shape=[jax.ShapeDtypeStruct((TOP_K, t), jnp.int32), jax.ShapeDtypeStruct((TOP_K, t), F32)],
        compiler_params=_params(),
        name="router_top2",
    )(x, router_t)


def _gather_rows(idx_ref, n, src_hbm, dst, sem):
    def body(r, carry):
        pltpu.make_async_copy(src_hbm.at[pl.ds(idx_ref[0, 0, r], 1), :], dst.at[pl.ds(r, 1), :], sem).start()
        return carry
    lax.fori_loop(0, n, body, 0, unroll=8)


def _wait_rows(src_hbm, dst, sem):
    pltpu.make_async_copy(src_hbm.at[pl.ds(0, dst.shape[0]), :], dst, sem).wait()


def _moe_kernel(tile_e_ref, nvalid_ref, idx_cur_ref, idx_next_ref, x_hbm, wg_ref, wu_ref, wd_ref,
                y_ref, xbuf, sem, h_ref):
    del tile_e_ref
    j = pl.program_id(0)
    nv = nvalid_ref[0]
    tm = xbuf.shape[1]
    slot = j % 2

    @pl.when(j == 0)
    def _():
        _gather_rows(idx_cur_ref, tm, x_hbm, xbuf.at[0], sem.at[0])

    @pl.when(j + 1 < nv)
    def _():
        _gather_rows(idx_next_ref, tm, x_hbm, xbuf.at[1 - slot], sem.at[1 - slot])

    @pl.when(j < nv)
    def _():
        _wait_rows(x_hbm, xbuf.at[slot], sem.at[slot])
        y_ref[...] = _swiglu_tile(xbuf[slot].astype(BF16), wg_ref, wu_ref, wd_ref, h_ref)

    @pl.when(j >= nv)
    def _():
        y_ref[...] = jnp.zeros_like(y_ref)


def _moe_experts(x, tok_sorted, tile_e, nvalid, wg, wu, wd):
    tm = ROW_TILE
    n_tiles = tok_sorted.shape[0]
    wspec = lambda shape: pl.BlockSpec((1,) + shape, lambda j, te, nv: (te[j], 0, 0))
    idx_shape = (1, 1, tm)
    return pl.pallas_call(
        _moe_kernel,
        grid_spec=pltpu.PrefetchScalarGridSpec(
            num_scalar_prefetch=2,
            grid=(n_tiles,),
            in_specs=[
                pl.BlockSpec(idx_shape, lambda j, te, nv: (j, 0, 0), memory_space=pltpu.SMEM),
                pl.BlockSpec(idx_shape, lambda j, te, nv: (jnp.minimum(j + 1, n_tiles - 1), 0, 0),
                             memory_space=pltpu.SMEM),
                pl.BlockSpec(memory_space=pl.ANY),
                wspec((D_MODEL, D_FF)), wspec((D_MODEL, D_FF)), wspec((D_FF, D_MODEL)),
            ],
            out_specs=pl.BlockSpec((tm, D_MODEL), lambda j, te, nv: (j, 0)),
            scratch_shapes=[
                pltpu.VMEM((2, tm, D_MODEL), F32),
                pltpu.SemaphoreType.DMA((2,)),
                pltpu.VMEM((tm, D_FF), BF16),
            ],
        ),
        out_shape=jax.ShapeDtypeStruct((n_tiles * tm, D_MODEL), F32),
        compiler_params=_params(vmem=V7X_VMEM_LIMIT_BYTES),
        name="moe_experts",
    )(tile_e, nvalid, tok_sorted, tok_sorted, x, wg, wu, wd)


def _combine_kernel(pos_cur_ref, pos_next_ref, x_ref, w_ref, y_hbm, g_ref, b_ref, o_ref, ybuf, sem, *, alpha):
    j = pl.program_id(0)
    n = pl.num_programs(0)
    slot = j % 2
    rows = ybuf.shape[1]

    @pl.when(j == 0)
    def _():
        _gather_rows(pos_cur_ref, rows, y_hbm, ybuf.at[0], sem.at[0])

    @pl.when(j + 1 < n)
    def _():
        _gather_rows(pos_next_ref, rows, y_hbm, ybuf.at[1 - slot], sem.at[1 - slot])

    _wait_rows(y_hbm, ybuf.at[slot], sem.at[slot])
    tc = rows // TOP_K
    w = w_ref[...]
    f = w[:, 0:1] * ybuf[slot, 0:tc, :] + w[:, 1:2] * ybuf[slot, tc:2 * tc, :]
    o_ref[...] = _layernorm(alpha * x_ref[...] + f, g_ref[...], b_ref[...])


def _moe_combine(x, gate_w, pos_tiles, y_sorted, g, b, alpha):
    t = x.shape[0]
    tc = COMBINE_TILE
    n = t // tc
    pos_shape = (1, 1, TOP_K * tc)
    return pl.pallas_call(
        functools.partial(_combine_kernel, alpha=alpha),
        grid=(n,),
        in_specs=[
            pl.BlockSpec(pos_shape, lambda j: (j, 0, 0), memory_space=pltpu.SMEM),
            pl.BlockSpec(pos_shape, lambda j: (jnp.minimum(j + 1, n - 1), 0, 0), memory_space=pltpu.SMEM),
            pl.BlockSpec((tc, D_MODEL), lambda j: (j, 0)),
            pl.BlockSpec((tc, TOP_K), lambda j: (j, 0)),
            pl.BlockSpec(memory_space=pl.ANY),
            pl.BlockSpec((1, D_MODEL), lambda j: (0, 0)),
            pl.BlockSpec((1, D_MODEL), lambda j: (0, 0)),
        ],
        out_specs=pl.BlockSpec((tc, D_MODEL), lambda j: (j, 0)),
        out_shape=jax.ShapeDtypeStruct((t, D_MODEL), F32),
        scratch_shapes=[pltpu.VMEM((2, TOP_K * tc, D_MODEL), F32), pltpu.SemaphoreType.DMA((2,))],
        compiler_params=_params(),
        name="moe_combine_ln2",
    )(pos_tiles, pos_tiles, x, gate_w, y_sorted, g, b)


def _route(idx_t, tm, tc):
    t = idx_t.shape[0]
    a = t * TOP_K
    e_flat = idx_t.reshape(a)
    onehot = (e_flat[:, None] == jnp.arange(N_EXPERTS, dtype=jnp.int32)[None, :]).astype(jnp.int32)
    csum = jnp.cumsum(onehot, axis=0)
    rank = jnp.sum(onehot * csum, axis=1) - 1
    counts = csum[-1]
    padded = ((counts + tm - 1) // tm) * tm
    ends = jnp.cumsum(padded)
    pos = jnp.sum(onehot * (ends - padded)[None, :], axis=1) + rank
    n_tiles = a // tm + N_EXPERTS
    tok_sorted = jnp.zeros((n_tiles * tm,), jnp.int32).at[pos].set(jnp.arange(a, dtype=jnp.int32) // TOP_K)
    nvalid = (ends[-1] // tm).astype(jnp.int32)
    starts = jnp.arange(n_tiles, dtype=jnp.int32) * tm
    tile_e = jnp.minimum(jnp.sum((starts[:, None] >= ends[None, :]).astype(jnp.int32), axis=1), N_EXPERTS - 1)
    tile_e = jnp.where(jnp.arange(n_tiles) < nvalid, tile_e, tile_e[nvalid - 1])
    pos_tiles = pos.reshape(t // tc, tc, TOP_K).transpose(0, 2, 1).reshape(t // tc, 1, TOP_K * tc)
    return tok_sorted.reshape(n_tiles, 1, tm), tile_e, nvalid.reshape(1), pos_tiles


def _moe_layer(x, router_t, wg, wu, wd, g, b, alpha):
    idx, gate_w = _router(x, router_t)
    tok_sorted, tile_e, nvalid, pos_tiles = _route(idx.T, ROW_TILE, COMBINE_TILE)
    y_sorted = _moe_experts(x, tok_sorted, tile_e, nvalid, wg, wu, wd)
    return _moe_combine(x, gate_w.T, pos_tiles, y_sorted, g, b, alpha)


def _segment_tables(groups, unit):
    prev, nxt, within, per_seq = [], [], [], []
    base = 0
    for n_seq, seq_len in groups:
        nb = seq_len // unit
        for _ in range(n_seq):
            for n in range(nb):
                i = base + n
                prev.append(i - 1 if n > 0 else i)
                nxt.append(i + 1 if n < nb - 1 else i)
                within.append(n)
                per_seq.append(nb)
            base += nb
    as_i32 = lambda v: jnp.asarray(np.asarray(v, np.int32))
    return as_i32(prev), as_i32(nxt), as_i32(within), as_i32(per_seq)


def _window_tables(groups):
    prev, nxt, within, per_seq = _segment_tables(groups, ROW_TILE)
    per_tile = ROW_TILE // WBLK
    tile = jnp.arange(prev.shape[0], dtype=jnp.int32)
    first = (within == 0).astype(jnp.int32)
    last = (within == per_seq - 1).astype(jnp.int32)
    prev_blk = jnp.where(first == 1, tile * per_tile, tile * per_tile - 1)
    next_blk = jnp.where(last == 1, tile * per_tile + per_tile - 1, tile * per_tile + per_tile)
    i = np.arange(WBLK)[:, None]
    jj = np.arange(3 * WBLK)[None, :]
    band = np.abs(jj - WBLK - i) <= WINDOW
    masks = []
    for no_prev, no_next in ((True, False), (False, False), (False, True)):
        inr = np.ones_like(band)
        if no_prev:
            inr = inr & (jj >= WBLK)
        if no_next:
            inr = inr & (jj < 2 * WBLK)
        masks.append(np.where(band & inr, 0.0, NEG))
    return prev_blk, next_blk, first, last, jnp.asarray(np.stack(masks).astype(np.float32))


def _nbr_tables(groups):
    chunk = NA_CHUNK_ROWS * GRID_W
    prev, nxt, within, per_seq = _segment_tables(groups, chunk)
    return prev, nxt, within * NA_CHUNK_ROWS, per_seq * NA_CHUNK_ROWS


def _nbr_bias_pairs(na_rpb):
    c = np.arange(GRID_W)[:, None]
    kc = np.arange(GRID_W)[None, :]
    cstart = np.clip(c - NA_COLS // 2, 0, GRID_W - NA_COLS)
    allowed = (kc >= cstart) & (kc < cstart + NA_COLS)
    dc = np.clip(kc - c + (NA_COLS - 1), 0, 2 * NA_COLS - 2)
    full = jnp.where(jnp.asarray(allowed)[None, None, None], na_rpb.astype(F32)[:, :, :, dc], NEG)
    pairs = jnp.concatenate([full[:, :, :-1], full[:, :, 1:]], axis=-1)
    nl, nh, nd = pairs.shape[:3]
    pairs = pairs.reshape(nl, nh // 2, 2, nd, GRID_W, 2 * GRID_W).transpose(0, 1, 3, 2, 4, 5)
    return pairs.reshape(nl, nh // 2, nd, 2 * GRID_W, 2 * GRID_W)


def _rope_tables(max_len):
    half = HEAD_DIM // 2
    inv = 1.0 / (ROPE_THETA ** (jnp.arange(half, dtype=F32) / half))
    ang = jnp.arange(max_len, dtype=F32)[:, None] * inv[None, :]
    cos, sin = jnp.cos(ang), jnp.sin(ang)
    reps = LANES // HEAD_DIM
    return jnp.tile(jnp.concatenate([cos, cos], axis=-1), (1, reps)), \
        jnp.tile(jnp.concatenate([-sin, sin], axis=-1), (1, reps))


def _trunk(xa, xb, groups, emb_ln_g, emb_ln_b, w_in, attn_sink, na_rpb, gnorm_a, gnorm_b, w_o, ln1_g, ln1_b,
           ffn_gate, ffn_up, ffn_down, router, exp_gate, exp_up, exp_down, ln2_g, ln2_b):
    depth = w_in.shape[0]
    alpha = (2.0 * depth) ** 0.25
    for n_seq, seq_len in groups:
        assert seq_len % (NA_CHUNK_ROWS * GRID_W) == 0 and seq_len // WBLK >= 2
        assert (n_seq * seq_len) % ROW_TILE == 0 and seq_len % ROW_TILE == 0
    row2 = lambda v: v.reshape(1, -1)

    win_tables = _window_tables(groups)
    nbr_tables = _nbr_tables(groups)
    cos_t, sin_t = _rope_tables(max(s for _, s in groups))
    _, _, pos_within, _ = _segment_tables(groups, ROW_TILE)
    bias_pairs = _nbr_bias_pairs(na_rpb)
    head_order = np.arange(H_A).reshape(HKV_A, GQA).T.reshape(-1)
    qa_cols = (head_order[:, None] * HEAD_DIM + np.arange(HEAD_DIM)[None, :]).reshape(-1)
    in_cols = np.concatenate([qa_cols, np.arange(QA_W, IN_W)])
    o_rows = np.concatenate([qa_cols, np.arange(QA_W, QA_W + B_W)])
    w_in_b = w_in[:, :, in_cols].astype(BF16)
    w_o_b = w_o[:, o_rows, :].astype(BF16)
    gnorm_a = gnorm_a[:, qa_cols]
    ffn_b = [w.astype(BF16) for w in (ffn_gate, ffn_up, ffn_down)]
    exp_b = [w.astype(BF16) for w in (exp_gate, exp_up, exp_down)]
    router_t = jnp.swapaxes(router, 1, 2)

    x = _embln(xa, xb, row2(emb_ln_g), row2(emb_ln_b))
    for l in range(depth):
        qa, ka, va, qb, kb, vb = _inproj(x, w_in_b[l], cos_t, sin_t, pos_within)
        oa = _window_attn(qa, ka, va, attn_sink[l], row2(gnorm_a[l]), win_tables)
        ob = _nbr_attn(qb, kb, vb, bias_pairs[l], row2(gnorm_b[l]), nbr_tables)
        x = _outproj(x, oa, ob, w_o_b[l, :QA_W], w_o_b[l, QA_W:], row2(ln1_g[l]), row2(ln1_b[l]), alpha)
        i = l // 2
        if l % 2 == 0:
            x = _ffn_dense(x, ffn_b[0][i:i + 1], ffn_b[1][i:i + 1], ffn_b[2][i:i + 1],
                           row2(ln2_g[l]), row2(ln2_b[l]), alpha)
        else:
            x = _moe_layer(x, router_t[i], exp_b[0][i], exp_b[1][i], exp_b[2][i],
                           row2(ln2_g[l]), row2(ln2_b[l]), alpha)
    return x


def kernel(x_prompt, x_sample, emb_ln_g, emb_ln_b, w_in, attn_sink, na_rpb, gnorm_a, gnorm_b, w_o, ln1_g, ln1_b,
           ffn_gate, ffn_up, ffn_down, router, exp_gate, exp_up, exp_down, ln2_g, ln2_b):
    groups = (x_prompt.shape[:2], x_sample.shape[:2])
    ta = x_prompt.shape[0] * x_prompt.shape[1]
    y = _trunk(x_prompt.reshape(ta, D_MODEL), x_sample.reshape(-1, D_MODEL), groups,
               emb_ln_g, emb_ln_b, w_in, attn_sink, na_rpb, gnorm_a, gnorm_b, w_o, ln1_g, ln1_b,
               ffn_gate, ffn_up, ffn_down, router, exp_gate, exp_up, exp_down, ln2_g, ln2_b)
    return y[:ta].reshape(x_prompt.shape), y[ta:].reshape(x_sample.shape)
```

```python
import functools

import numpy as np
import jax
import jax.numpy as jnp
from jax import lax
from jax.experimental import pallas as pl
from jax.experimental.pallas import tpu as pltpu

D_MODEL = 1024
HEAD_DIM = 64
H_A = 8
HKV_A = 2
GQA = H_A // HKV_A
H_B = 8
WINDOW = 128
WBLK = 128
ROPE_THETA = 10000.0
GRID_W = 64
NA_ROWS = 8
NA_COLS = 16
D_FF = 2816
N_EXPERTS = 8
TOP_K = 2
LN_EPS = 1e-5
NEG = -1e30
QA_W = H_A * HEAD_DIM
KVA_W = HKV_A * HEAD_DIM
B_W = H_B * HEAD_DIM
IN_W = QA_W + 2 * KVA_W + 3 * B_W
ROPE_W = QA_W + KVA_W

V7X_VMEM_LIMIT_BYTES = 56 * 1024 * 1024
LANES = 128
MXU_N = 256
ROW_TILE = 512
NA_CHUNK_ROWS = 8
MOE_TILE = 512
SLAB_ALIGN = 16
SLAB_ROWS = 1152

F32 = jnp.float32
BF16 = jnp.bfloat16


def _layernorm(y, g, b):
    mu = jnp.mean(y, axis=-1, keepdims=True)
    yc = y - mu
    var = jnp.mean(yc * yc, axis=-1, keepdims=True)
    return yc * lax.rsqrt(var + LN_EPS) * g + b


def _rmsnorm(o, g):
    ms = jnp.mean(o * o, axis=-1, keepdims=True)
    return o * lax.rsqrt(ms + LN_EPS) * g


def _dot_nt(a, b):
    return lax.dot_general(a, b, (((1,), (1,)), ((), ())), preferred_element_type=F32)


def _params(n_axes=1, vmem=None):
    return pltpu.CompilerParams(dimension_semantics=("arbitrary",) * n_axes, vmem_limit_bytes=vmem)


def _embln_kernel(xa_ref, xb_ref, g_ref, b_ref, o_ref, *, n_a):
    i = pl.program_id(0)

    @pl.when(i < n_a)
    def _():
        o_ref[...] = _layernorm(xa_ref[...], g_ref[...], b_ref[...])

    @pl.when(i >= n_a)
    def _():
        o_ref[...] = _layernorm(xb_ref[...], g_ref[...], b_ref[...])


def _embln(xa, xb, g, b):
    ta, tb = xa.shape[0], xb.shape[0]
    tm = ROW_TILE
    n_a, n_b = ta // tm, tb // tm
    return pl.pallas_call(
        functools.partial(_embln_kernel, n_a=n_a),
        grid=(n_a + n_b,),
        in_specs=[
            pl.BlockSpec((tm, D_MODEL), lambda i: (jnp.minimum(i, n_a - 1), 0)),
            pl.BlockSpec((tm, D_MODEL), lambda i: (jnp.maximum(i - n_a, 0), 0)),
            pl.BlockSpec((1, D_MODEL), lambda i: (0, 0)),
            pl.BlockSpec((1, D_MODEL), lambda i: (0, 0)),
        ],
        out_specs=pl.BlockSpec((tm, D_MODEL), lambda i: (i, 0)),
        out_shape=jax.ShapeDtypeStruct((ta + tb, D_MODEL), F32),
        compiler_params=_params(),
        name="emb_ln",
    )(xa, xb, g, b)


def _inproj_kernel(pos_ref, x_ref, w_ref, cos_ref, sin_ref,
                   qa_ref, ka_ref, va_ref, qb_ref, kb_ref, vb_ref):
    del pos_ref
    xb = x_ref[...].astype(BF16)
    cos = cos_ref[...]
    sin = sin_ref[...]
    lane = lax.broadcasted_iota(jnp.int32, cos.shape, 1)
    first_half = (lane % HEAD_DIM) < (HEAD_DIM // 2)
    scale = HEAD_DIM ** -0.5

    def rope(h):
        partner = jnp.where(first_half, pltpu.roll(h, LANES - HEAD_DIM // 2, 1),
                            pltpu.roll(h, HEAD_DIM // 2, 1))
        return h * cos + partner * sin

    for c in range(ROPE_W // LANES):
        h = jnp.dot(xb, w_ref[:, c * LANES:(c + 1) * LANES], preferred_element_type=F32)
        r = rope(h)
        if c < QA_W // LANES:
            qa_ref[:, c * LANES:(c + 1) * LANES] = (r * scale).astype(BF16)
        else:
            ka_ref[...] = r.astype(BF16)
    off = ROPE_W
    va_ref[...] = jnp.dot(xb, w_ref[:, off:off + KVA_W], preferred_element_type=F32).astype(BF16)
    off += KVA_W
    for j, (ref, s) in enumerate(((qb_ref, scale), (kb_ref, 1.0), (vb_ref, 1.0))):
        for c in range(B_W // MXU_N):
            lo = off + j * B_W + c * MXU_N
            h = jnp.dot(xb, w_ref[:, lo:lo + MXU_N], preferred_element_type=F32)
            ref[:, c * MXU_N:(c + 1) * MXU_N] = (h * s).astype(BF16)


def _inproj(x, w_bf16, cos_t, sin_t, pos_blk):
    t = x.shape[0]
    tm = ROW_TILE
    row = lambda i, p: (i, 0)
    out_shapes = [jax.ShapeDtypeStruct((t, w), BF16) for w in (QA_W, KVA_W, KVA_W, B_W, B_W, B_W)]
    return pl.pallas_call(
        _inproj_kernel,
        grid_spec=pltpu.PrefetchScalarGridSpec(
            num_scalar_prefetch=1,
            grid=(t // tm,),
            in_specs=[
                pl.BlockSpec((tm, D_MODEL), row),
                pl.BlockSpec((D_MODEL, IN_W), lambda i, p: (0, 0)),
                pl.BlockSpec((tm, LANES), lambda i, p: (p[i], 0)),
                pl.BlockSpec((tm, LANES), lambda i, p: (p[i], 0)),
            ],
            out_specs=[pl.BlockSpec((tm, s.shape[1]), row) for s in out_shapes],
        ),
        out_shape=out_shapes,
        compiler_params=_params(vmem=V7X_VMEM_LIMIT_BYTES),
        name="in_proj_rope",
    )(pos_blk, x, w_bf16, cos_t, sin_t)


def _split_heads(qc):
    lo = lax.broadcasted_iota(jnp.int32, qc.shape, 1) < HEAD_DIM
    zero = jnp.zeros_like(qc)
    return jnp.concatenate([jnp.where(lo, qc, zero), jnp.where(lo, zero, qc)], axis=0)


def _merge_heads(o2):
    m = o2.shape[0] // 2
    lo = lax.broadcasted_iota(jnp.int32, (m, LANES), 1) < HEAD_DIM
    return jnp.where(lo, o2[:m], o2[m:])


def _win_kernel(prev_ref, next_ref, first_ref, last_ref, sink_ref, q_ref, kp_ref, kc_ref, kn_ref,
                vp_ref, vc_ref, vn_ref, bias_ref, g_ref, o_ref, kwin, vwin):
    del prev_ref, next_ref
    i = pl.program_id(0)
    tq = q_ref.shape[0]
    nsub = tq // WBLK
    kwin[0:WBLK, :] = kp_ref[...]
    kwin[WBLK:WBLK + tq, :] = kc_ref[...]
    kwin[WBLK + tq:, :] = kn_ref[...]
    vwin[0:WBLK, 0:KVA_W] = vp_ref[...]
    vwin[WBLK:WBLK + tq, 0:KVA_W] = vc_ref[...]
    vwin[WBLK + tq:, 0:KVA_W] = vn_ref[...]
    vwin[:, KVA_W:] = jnp.ones((vwin.shape[0], LANES), BF16)
    first = first_ref[i]
    last = last_ref[i]
    g = g_ref[...]
    upper = lax.broadcasted_iota(jnp.int32, (2 * WBLK, 1), 0) < WBLK

    def body(j, carry):
        off = pl.multiple_of(j * WBLK, WBLK)
        bt = jnp.where((j == 0) & (first == 1), 0, jnp.where((j == nsub - 1) & (last == 1), 2, 1))
        bias = bias_ref[bt]
        bias2 = jnp.concatenate([bias, bias], axis=0)
        k3 = kwin[pl.ds(off, 3 * WBLK), :]
        v3 = vwin[pl.ds(off, 3 * WBLK), :]
        q = q_ref[pl.ds(off, WBLK), :]
        outs = []
        for c in range(QA_W // LANES):
            s = _dot_nt(_split_heads(q[:, c * LANES:(c + 1) * LANES]), k3) + bias2
            sk = jnp.where(upper, sink_ref[c], sink_ref[GQA + c])
            m = jnp.maximum(jnp.max(s, axis=-1, keepdims=True), sk)
            p = jnp.exp(s - m).astype(BF16)
            of = jnp.dot(p, v3, preferred_element_type=F32)
            den = of[:, LANES:] + jnp.exp(sk - m)
            outs.append(_merge_heads(of[:, :LANES] / den))
        o = jnp.concatenate(outs, axis=-1)
        o_ref[pl.ds(off, WBLK), :] = _rmsnorm(o, g).astype(BF16)
        return carry

    lax.fori_loop(0, nsub, body, 0, unroll=True)


def _window_attn(qa, ka, va, sink, gnorm, tables):
    t = qa.shape[0]
    tq = ROW_TILE
    prev_blk, next_blk, first, last, wbias = tables
    cur = pl.BlockSpec((tq, KVA_W), lambda i, pv, nx, fi, la: (i, 0))
    prv = pl.BlockSpec((WBLK, KVA_W), lambda i, pv, nx, fi, la: (pv[i], 0))
    nxt = pl.BlockSpec((WBLK, KVA_W), lambda i, pv, nx, fi, la: (nx[i], 0))
    return pl.pallas_call(
        _win_kernel,
        grid_spec=pltpu.PrefetchScalarGridSpec(
            num_scalar_prefetch=4,
            grid=(t // tq,),
            in_specs=[
                pl.BlockSpec(memory_space=pltpu.SMEM),
                pl.BlockSpec((tq, QA_W), lambda i, pv, nx, fi, la: (i, 0)),
                prv, cur, nxt, prv, cur, nxt,
                pl.BlockSpec(wbias.shape, lambda i, pv, nx, fi, la: (0, 0, 0)),
                pl.BlockSpec((1, QA_W), lambda i, pv, nx, fi, la: (0, 0)),
            ],
            out_specs=pl.BlockSpec((tq, QA_W), lambda i, pv, nx, fi, la: (i, 0)),
            scratch_shapes=[pltpu.VMEM((tq + 2 * WBLK, KVA_W), BF16),
                            pltpu.VMEM((tq + 2 * WBLK, KVA_W + LANES), BF16)],
        ),
        out_shape=jax.ShapeDtypeStruct((t, QA_W), BF16),
        compiler_params=_params(),
        name="window_attn",
    )(prev_blk, next_blk, first, last, sink, qa, ka, ka, ka, va, va, va, wbias, gnorm)


def _nbr_kernel(prev_ref, next_ref, row0_ref, rows_ref, q_ref, kp_ref, kc_ref, kn_ref,
                vp_ref, vc_ref, vn_ref, bias_ref, g_ref, o_ref, kwin, vwin):
    del prev_ref, next_ref
    j = pl.program_id(0)
    chunk = NA_CHUNK_ROWS * GRID_W
    ones = jnp.ones((3 * chunk, LANES), BF16)
    for c in range(B_W // LANES):
        vwin[:, (2 * c + 1) * LANES:(2 * c + 2) * LANES] = ones
    for w, (kr, vr) in enumerate(((kp_ref, vp_ref), (kc_ref, vc_ref), (kn_ref, vn_ref))):
        kwin[w * chunk:(w + 1) * chunk, :] = kr[...]
        for c in range(B_W // LANES):
            vwin[w * chunk:(w + 1) * chunk, 2 * c * LANES:(2 * c + 1) * LANES] = vr[:, c * LANES:(c + 1) * LANES]
    row0 = row0_ref[j]
    rows = rows_ref[j]
    g = g_ref[...]
    nkeys = NA_ROWS * GRID_W

    def body(rr, carry):
        r = row0 + rr
        rstart = jnp.clip(r - NA_ROWS // 2, 0, rows - NA_ROWS)
        bidx = (NA_ROWS - 1) - (r - rstart)
        loc = pl.multiple_of((rstart - row0 + NA_CHUNK_ROWS) * GRID_W, GRID_W)
        qoff = pl.multiple_of(rr * GRID_W, GRID_W)
        q = q_ref[pl.ds(qoff, GRID_W), :]
        outs = []
        for c in range(B_W // LANES):
            kc = kwin[pl.ds(loc, nkeys), c * LANES:(c + 1) * LANES]
            vc = vwin[pl.ds(loc, nkeys), 2 * c * LANES:(2 * c + 2) * LANES]
            s = _dot_nt(_split_heads(q[:, c * LANES:(c + 1) * LANES]), kc)
            s = s + jnp.concatenate([bias_ref[c, bidx + 2 * jj] for jj in range(NA_ROWS // 2)], axis=-1)
            m = jnp.max(s, axis=-1, keepdims=True)
            p = jnp.exp(s - m).astype(BF16)
            of = jnp.dot(p, vc, preferred_element_type=F32)
            outs.append(_merge_heads(of[:, :LANES] / of[:, LANES:]))
        o = jnp.concatenate(outs, axis=-1)
        o_ref[pl.ds(qoff, GRID_W), :] = _rmsnorm(o, g).astype(BF16)
        return carry

    lax.fori_loop(0, NA_CHUNK_ROWS, body, 0, unroll=2)


def _nbr_attn(qb, kb, vb, bias_pairs, gnorm, tables):
    t = qb.shape[0]
    prev_c, next_c, row0, rows = tables
    chunk = NA_CHUNK_ROWS * GRID_W
    cur = pl.BlockSpec((chunk, B_W), lambda i, pv, nx, r0, rs: (i, 0))
    prv = pl.BlockSpec((chunk, B_W), lambda i, pv, nx, r0, rs: (pv[i], 0))
    nxt = pl.BlockSpec((chunk, B_W), lambda i, pv, nx, r0, rs: (nx[i], 0))
    return pl.pallas_call(
        _nbr_kernel,
        grid_spec=pltpu.PrefetchScalarGridSpec(
            num_scalar_prefetch=4,
            grid=(t // chunk,),
            in_specs=[
                cur, prv, cur, nxt, prv, cur, nxt,
                pl.BlockSpec(bias_pairs.shape, lambda i, pv, nx, r0, rs: (0, 0, 0, 0)),
                pl.BlockSpec((1, B_W), lambda i, pv, nx, r0, rs: (0, 0)),
            ],
            out_specs=cur,
            scratch_shapes=[pltpu.VMEM((3 * chunk, B_W), BF16), pltpu.VMEM((3 * chunk, 2 * B_W), BF16)],
        ),
        out_shape=jax.ShapeDtypeStruct((t, B_W), BF16),
        compiler_params=_params(vmem=V7X_VMEM_LIMIT_BYTES),
        name="nbr_attn",
    )(prev_c, next_c, row0, rows, qb, kb, kb, kb, vb, vb, vb, bias_pairs, gnorm)


def _outproj_kernel(x_ref, oa_ref, ob_ref, wa_ref, wb_ref, g_ref, b_ref, o_ref, *, alpha):
    mix = jnp.dot(oa_ref[...], wa_ref[...], preferred_element_type=F32)
    mix = mix + jnp.dot(ob_ref[...], wb_ref[...], preferred_element_type=F32)
    o_ref[...] = _layernorm(alpha * x_ref[...] + mix, g_ref[...], b_ref[...])


def _outproj(x, oa, ob, wa, wb, g, b, alpha):
    t = x.shape[0]
    tm = ROW_TILE
    row = lambda i: (i, 0)
    const = lambda i: (0, 0)
    return pl.pallas_call(
        functools.partial(_outproj_kernel, alpha=alpha),
        grid=(t // tm,),
        in_specs=[
            pl.BlockSpec((tm, D_MODEL), row),
            pl.BlockSpec((tm, QA_W), row),
            pl.BlockSpec((tm, B_W), row),
            pl.BlockSpec((QA_W, D_MODEL), const),
            pl.BlockSpec((B_W, D_MODEL), const),
            pl.BlockSpec((1, D_MODEL), const),
            pl.BlockSpec((1, D_MODEL), const),
        ],
        out_specs=pl.BlockSpec((tm, D_MODEL), row),
        out_shape=jax.ShapeDtypeStruct((t, D_MODEL), F32),
        compiler_params=_params(vmem=V7X_VMEM_LIMIT_BYTES),
        name="out_proj_ln1",
    )(x, oa, ob, wa, wb, g, b)


def _swiglu_tile(xb, wg_ref, wu_ref, wd_ref, h_ref):
    for c in range(D_FF // MXU_N):
        sl = slice(c * MXU_N, (c + 1) * MXU_N)
        gate = jnp.dot(xb, wg_ref[0, :, sl], preferred_element_type=F32)
        up = jnp.dot(xb, wu_ref[0, :, sl], preferred_element_type=F32)
        h_ref[:, sl] = (gate / (1.0 + jnp.exp(-gate)) * up).astype(BF16)
    return jnp.dot(h_ref[...], wd_ref[0], preferred_element_type=F32)


def _ffn_kernel(x_ref, wg_ref, wu_ref, wd_ref, g_ref, b_ref, o_ref, h_ref, *, alpha):
    x = x_ref[...]
    f = _swiglu_tile(x.astype(BF16), wg_ref, wu_ref, wd_ref, h_ref)
    o_ref[...] = _layernorm(alpha * x + f, g_ref[...], b_ref[...])


def _ffn_dense(x, wg, wu, wd, g, b, alpha):
    t = x.shape[0]
    tm = ROW_TILE
    row = lambda i: (i, 0)
    const = lambda i: (0, 0)
    const3 = lambda i: (0, 0, 0)
    return pl.pallas_call(
        functools.partial(_ffn_kernel, alpha=alpha),
        grid=(t // tm,),
        in_specs=[
            pl.BlockSpec((tm, D_MODEL), row),
            pl.BlockSpec((1, D_MODEL, D_FF), const3),
            pl.BlockSpec((1, D_MODEL, D_FF), const3),
            pl.BlockSpec((1, D_FF, D_MODEL), const3),
            pl.BlockSpec((1, D_MODEL), const),
            pl.BlockSpec((1, D_MODEL), const),
        ],
        out_specs=pl.BlockSpec((tm, D_MODEL), row),
        out_shape=jax.ShapeDtypeStruct((t, D_MODEL), F32),
        scratch_shapes=[pltpu.VMEM((tm, D_FF), BF16)],
        compiler_params=_params(vmem=V7X_VMEM_LIMIT_BYTES),
        name="ffn_dense_ln2",
    )(x, wg, wu, wd, g, b)


def _router_kernel(x_ref, rt_ref, idx_ref, w_ref):
    logits = lax.dot_general(rt_ref[...], x_ref[...], (((1,), (1,)), ((), ())),
                             precision=lax.Precision.HIGHEST, preferred_element_type=F32)
    e = lax.broadcasted_iota(jnp.int32, logits.shape, 0)
    m1 = jnp.max(logits, axis=0, keepdims=True)
    i1 = jnp.min(jnp.where(logits == m1, e, N_EXPERTS), axis=0, keepdims=True)
    rest = jnp.where(e == i1, -jnp.inf, logits)
    m2 = jnp.max(rest, axis=0, keepdims=True)
    i2 = jnp.min(jnp.where(rest == m2, e, N_EXPERTS), axis=0, keepdims=True)
    t2 = jnp.exp(m2 - m1)
    idx_ref[...] = jnp.concatenate([i1, i2], axis=0)
    w_ref[...] = jnp.concatenate([1.0 / (1.0 + t2), t2 / (1.0 + t2)], axis=0)


def _router(x, router_t):
    t = x.shape[0]
    tm = ROW_TILE
    return pl.pallas_call(
        _router_kernel,
        grid=(t // tm,),
        in_specs=[
            pl.BlockSpec((tm, D_MODEL), lambda i: (i, 0)),
            pl.BlockSpec((N_EXPERTS, D_MODEL), lambda i: (0, 0)),
        ],
        out_specs=[pl.BlockSpec((TOP_K, tm), lambda i: (0, i)), pl.BlockSpec((TOP_K, tm), lambda i: (0, i))],
        out_shape=[jax.ShapeDtypeStruct((TOP_K, t), jnp.int32), jax.ShapeDtypeStruct((TOP_K, t), F32)],
        compiler_params=_params(),
        name="router_top2",
    )(x, router_t)


def _slab_copies(n_rows, src, src_row, dst, dst_row, sem, action, max_rows):
    shift = SLAB_ALIGN.bit_length() - 1
    for b in reversed(range((max_rows // SLAB_ALIGN).bit_length())):
        size = SLAB_ALIGN << b
        if size > max_rows:
            continue
        done = (n_rows >> (shift + b + 1)) << (shift + b + 1)

        @pl.when(((n_rows >> (shift + b)) & 1) == 1)
        def _(size=size, done=done):
            s = pl.multiple_of(src_row + done, SLAB_ALIGN)
            d = pl.multiple_of(dst_row + done, SLAB_ALIGN)
            action(pltpu.make_async_copy(src.at[pl.ds(s, size)], dst.at[pl.ds(d, size)], sem))


def _start(copy):
    copy.start()


def _wait(copy):
    copy.wait()


def _dispatch_kernel(n_ref, off_ref, loff_ref, padoff_ref, padlen_ref, nvalid_ref, x_ref, lp_ref, xs_hbm,
                     slab, zeros, sems):
    i = pl.program_id(0)
    sem = sems.at[0]

    @pl.when(i == 0)
    def _():
        zeros[...] = jnp.zeros_like(zeros)
        tile_rows = zeros.shape[0]
        for action in (_start, _wait):
            for e in range(N_EXPERTS):
                _slab_copies(padlen_ref[e], zeros, 0, xs_hbm, padoff_ref[e], sem, action, tile_rows)

            def unused_tile(j, carry, action=action):
                row = pl.multiple_of(j * tile_rows, tile_rows)
                action(pltpu.make_async_copy(zeros, xs_hbm.at[pl.ds(row, tile_rows)], sem))
                return carry
            lax.fori_loop(nvalid_ref[0], xs_hbm.shape[0] // tile_rows, unused_tile, 0)

    lp = lp_ref[0]
    q = lax.broadcasted_iota(jnp.int32, (slab.shape[0], lp.shape[1]), 0)
    sel = jnp.where(q == lp[0:1, :], 1.0, jnp.where(q == lp[1:2, :], 1.0, 0.0))
    slab[...] = jnp.dot(sel.astype(BF16), x_ref[...].astype(BF16), preferred_element_type=F32).astype(BF16)
    for action in (_start, _wait):
        for e in range(N_EXPERTS):
            k = i * N_EXPERTS + e
            _slab_copies(n_ref[k], slab, loff_ref[k], xs_hbm, off_ref[k], sem, action, MOE_TILE)


def _moe_dispatch(x, route, n_rows):
    t = x.shape[0]
    tt = MOE_TILE
    imap2 = lambda i, *_: (i, 0)
    return pl.pallas_call(
        _dispatch_kernel,
        grid_spec=pltpu.PrefetchScalarGridSpec(
            num_scalar_prefetch=6,
            grid=(t // tt,),
            in_specs=[
                pl.BlockSpec((tt, D_MODEL), imap2),
                pl.BlockSpec((1, TOP_K, tt), lambda i, *_: (i, 0, 0)),
            ],
            out_specs=pl.BlockSpec(memory_space=pl.ANY),
            scratch_shapes=[
                pltpu.VMEM((SLAB_ROWS, D_MODEL), BF16),
                pltpu.VMEM((ROW_TILE, D_MODEL), BF16),
                pltpu.SemaphoreType.DMA((1,)),
            ],
        ),
        out_shape=jax.ShapeDtypeStruct((n_rows, D_MODEL), BF16),
        compiler_params=_params(vmem=V7X_VMEM_LIMIT_BYTES),
        name="moe_dispatch",
    )(route["n"], route["off"], route["loff"], route["pad_off"], route["pad_len"], route["nvalid"], x,
      route["lp_rows"])


def _moe_kernel(tile_e_ref, nvalid_ref, x_ref, wg_ref, wu_ref, wd_ref, y_ref, h_ref):
    del tile_e_ref
    j = pl.program_id(0)
    nv = nvalid_ref[0]

    @pl.when(j < nv)
    def _():
        y_ref[...] = _swiglu_tile(x_ref[...], wg_ref, wu_ref, wd_ref, h_ref).astype(BF16)

    @pl.when(j >= nv)
    def _():
        y_ref[...] = jnp.zeros_like(y_ref)


def _moe_experts(xs, tile_e, nvalid, wg, wu, wd):
    tm = ROW_TILE
    n_tiles = xs.shape[0] // tm
    wspec = lambda shape: pl.BlockSpec((1,) + shape, lambda j, te, nv: (te[j], 0, 0))
    return pl.pallas_call(
        _moe_kernel,
        grid_spec=pltpu.PrefetchScalarGridSpec(
            num_scalar_prefetch=2,
            grid=(n_tiles,),
            in_specs=[
                pl.BlockSpec((tm, D_MODEL), lambda j, te, nv: (jnp.minimum(j, nv[0] - 1), 0)),
                wspec((D_MODEL, D_FF)), wspec((D_MODEL, D_FF)), wspec((D_FF, D_MODEL)),
            ],
            out_specs=pl.BlockSpec((tm, D_MODEL), lambda j, te, nv: (j, 0)),
            scratch_shapes=[pltpu.VMEM((tm, D_FF), BF16)],
        ),
        out_shape=jax.ShapeDtypeStruct(xs.shape, BF16),
        compiler_params=_params(vmem=V7X_VMEM_LIMIT_BYTES),
        name="moe_experts",
    )(tile_e, nvalid, xs, wg, wu, wd)


def _combine_kernel(n_ref, off_ref, loff_ref, x_ref, lp_ref, w_ref, ys_hbm, g_ref, b_ref, *rest, alpha, split):
    out_refs, (yslab, sems) = rest[:-2], rest[-2:]
    i = pl.program_id(0)
    slot = i % 2

    def copies(tile, sl, action):
        for e in range(N_EXPERTS):
            k = tile * N_EXPERTS + e
            _slab_copies(n_ref[k], ys_hbm, off_ref[k], yslab.at[sl], loff_ref[k], sems.at[sl], action, MOE_TILE)

    @pl.when(i == 0)
    def _():
        yslab[...] = jnp.zeros_like(yslab)
        copies(0, 0, _start)

    @pl.when(i + 1 < pl.num_programs(0))
    def _():
        copies(i + 1, 1 - slot, _start)

    copies(i, slot, _wait)
    lp = lp_ref[...]
    w = w_ref[...]
    q = lax.broadcasted_iota(jnp.int32, (lp.shape[0], yslab.shape[1]), 1)
    gate = jnp.where(q == lp[:, 0:1], w[:, 0:1], 0.0) + jnp.where(q == lp[:, 1:2], w[:, 1:2], 0.0)
    f = jnp.dot(gate.astype(BF16), yslab[slot], preferred_element_type=F32)
    o = _layernorm(alpha * x_ref[...] + f, g_ref[...], b_ref[...])
    if split is None:
        out_refs[0][...] = o
    else:
        @pl.when(i < split)
        def _():
            out_refs[0][...] = o

        @pl.when(i >= split)
        def _():
            out_refs[1][...] = o


def _moe_combine(x, gate_w, route, y_sorted, g, b, alpha, split_rows=None):
    t = x.shape[0]
    tt = MOE_TILE
    n = t // tt
    imap2 = lambda i, *_: (i, 0)
    const = lambda i, *_: (0, 0)
    if split_rows is None:
        split = None
        out_specs = pl.BlockSpec((tt, D_MODEL), imap2)
        out_shape = jax.ShapeDtypeStruct((t, D_MODEL), F32)
    else:
        assert split_rows % tt == 0 and 0 < split_rows < t
        split = split_rows // tt
        out_specs = [pl.BlockSpec((tt, D_MODEL), lambda i, *_: (jnp.minimum(i, split - 1), 0)),
                     pl.BlockSpec((tt, D_MODEL), lambda i, *_: (jnp.maximum(i - split, 0), 0))]
        out_shape = [jax.ShapeDtypeStruct((split_rows, D_MODEL), F32),
                     jax.ShapeDtypeStruct((t - split_rows, D_MODEL), F32)]
    return pl.pallas_call(
        functools.partial(_combine_kernel, alpha=alpha, split=split),
        grid_spec=pltpu.PrefetchScalarGridSpec(
            num_scalar_prefetch=3,
            grid=(n,),
            in_specs=[
                pl.BlockSpec((tt, D_MODEL), imap2),
                pl.BlockSpec((tt, TOP_K), imap2),
                pl.BlockSpec((tt, TOP_K), imap2),
                pl.BlockSpec(memory_space=pl.ANY),
                pl.BlockSpec((1, D_MODEL), const),
                pl.BlockSpec((1, D_MODEL), const),
            ],
            out_specs=out_specs,
            scratch_shapes=[pltpu.VMEM((2, SLAB_ROWS, D_MODEL), BF16), pltpu.SemaphoreType.DMA((2,))],
        ),
        out_shape=out_shape,
        compiler_params=_params(vmem=V7X_VMEM_LIMIT_BYTES),
        name="moe_combine_ln2",
    )(route["n"], route["off"], route["loff"], x, route["lp_cols"], gate_w, y_sorted, g, b)


def _route(idx, t):
    tt = MOE_TILE
    ntt = t // tt
    i32 = jnp.int32
    e_tk = idx.T.reshape(ntt, tt * TOP_K)
    onehot = (e_tk[..., None] == jnp.arange(N_EXPERTS, dtype=i32)).astype(i32)
    csum = jnp.cumsum(onehot, axis=1)
    n = ((csum[:, -1, :] + SLAB_ALIGN - 1) // SLAB_ALIGN) * SLAB_ALIGN
    loff = jnp.cumsum(n, axis=1) - n
    lp = jnp.sum(onehot * (loff[:, None, :] + csum - 1), axis=-1)
    seg_rows = jnp.sum(n, axis=0)
    seg_len = ((seg_rows + ROW_TILE - 1) // ROW_TILE) * ROW_TILE
    ends = jnp.cumsum(seg_len)
    goff = ends - seg_len
    off = goff[None, :] + jnp.cumsum(n, axis=0) - n
    n_tiles = _sorted_rows(t) // ROW_TILE
    nvalid = ends[-1] // ROW_TILE
    starts = jnp.arange(n_tiles, dtype=i32) * ROW_TILE
    tile_e = jnp.minimum(jnp.sum((starts[:, None] >= ends[None, :]).astype(i32), axis=1), N_EXPERTS - 1)
    tile_e = jnp.where(jnp.arange(n_tiles) < nvalid, tile_e, tile_e[nvalid - 1])
    flat = lambda v: v.reshape(-1).astype(i32)
    return dict(n=flat(n), off=flat(off), loff=flat(loff), pad_off=flat(goff + seg_rows),
                pad_len=flat(seg_len - seg_rows), tile_e=flat(tile_e), nvalid=flat(nvalid),
                lp_rows=lp.reshape(ntt, tt, TOP_K).transpose(0, 2, 1).astype(i32),
                lp_cols=lp.reshape(t, TOP_K).astype(i32))


def _sorted_rows(t):
    worst = t * TOP_K + (t // MOE_TILE) * N_EXPERTS * (SLAB_ALIGN - 1) + N_EXPERTS * (ROW_TILE - 1)
    return -(-worst // ROW_TILE) * ROW_TILE


def _moe_layer(x, router_t, wg, wu, wd, g, b, alpha, split_rows=None):
    t = x.shape[0]
    idx, gate_w = _router(x, router_t)
    route = _route(idx, t)
    xs = _moe_dispatch(x, route, _sorted_rows(t))
    ys = _moe_experts(xs, route["tile_e"], route["nvalid"], wg, wu, wd)
    return _moe_combine(x, gate_w.T, route, ys, g, b, alpha, split_rows)


def _segment_tables(groups, unit):
    prev, nxt, within, per_seq = [], [], [], []
    base = 0
    for n_seq, seq_len in groups:
        nb = seq_len // unit
        for _ in range(n_seq):
            for n in range(nb):
                i = base + n
                prev.append(i - 1 if n > 0 else i)
                nxt.append(i + 1 if n < nb - 1 else i)
                within.append(n)
                per_seq.append(nb)
            base += nb
    as_i32 = lambda v: jnp.asarray(np.asarray(v, np.int32))
    return as_i32(prev), as_i32(nxt), as_i32(within), as_i32(per_seq)


def _window_tables(groups):
    prev, nxt, within, per_seq = _segment_tables(groups, ROW_TILE)
    per_tile = ROW_TILE // WBLK
    tile = jnp.arange(prev.shape[0], dtype=jnp.int32)
    first = (within == 0).astype(jnp.int32)
    last = (within == per_seq - 1).astype(jnp.int32)
    prev_blk = jnp.where(first == 1, tile * per_tile, tile * per_tile - 1)
    next_blk = jnp.where(last == 1, tile * per_tile + per_tile - 1, tile * per_tile + per_tile)
    i = np.arange(WBLK)[:, None]
    jj = np.arange(3 * WBLK)[None, :]
    band = np.abs(jj - WBLK - i) <= WINDOW
    masks = []
    for no_prev, no_next in ((True, False), (False, False), (False, True)):
        inr = np.ones_like(band)
        if no_prev:
            inr = inr & (jj >= WBLK)
        if no_next:
            inr = inr & (jj < 2 * WBLK)
        masks.append(np.where(band & inr, 0.0, NEG))
    return prev_blk, next_blk, first, last, jnp.asarray(np.stack(masks).astype(np.float32))


def _nbr_tables(groups):
    chunk = NA_CHUNK_ROWS * GRID_W
    prev, nxt, within, per_seq = _segment_tables(groups, chunk)
    return prev, nxt, within * NA_CHUNK_ROWS, per_seq * NA_CHUNK_ROWS


def _nbr_bias_pairs(na_rpb):
    c = np.arange(GRID_W)[:, None]
    kc = np.arange(GRID_W)[None, :]
    cstart = np.clip(c - NA_COLS // 2, 0, GRID_W - NA_COLS)
    allowed = (kc >= cstart) & (kc < cstart + NA_COLS)
    dc = np.clip(kc - c + (NA_COLS - 1), 0, 2 * NA_COLS - 2)
    full = jnp.where(jnp.asarray(allowed)[None, None, None], na_rpb.astype(F32)[:, :, :, dc], NEG)
    pairs = jnp.concatenate([full[:, :, :-1], full[:, :, 1:]], axis=-1)
    nl, nh, nd = pairs.shape[:3]
    pairs = pairs.reshape(nl, nh // 2, 2, nd, GRID_W, 2 * GRID_W).transpose(0, 1, 3, 2, 4, 5)
    return pairs.reshape(nl, nh // 2, nd, 2 * GRID_W, 2 * GRID_W)


def _rope_tables(max_len):
    half = HEAD_DIM // 2
    inv = 1.0 / (ROPE_THETA ** (jnp.arange(half, dtype=F32) / half))
    ang = jnp.arange(max_len, dtype=F32)[:, None] * inv[None, :]
    cos, sin = jnp.cos(ang), jnp.sin(ang)
    reps = LANES // HEAD_DIM
    return jnp.tile(jnp.concatenate([cos, cos], axis=-1), (1, reps)), \
        jnp.tile(jnp.concatenate([-sin, sin], axis=-1), (1, reps))


def _trunk(xa, xb, groups, emb_ln_g, emb_ln_b, w_in, attn_sink, na_rpb, gnorm_a, gnorm_b, w_o, ln1_g, ln1_b,
           ffn_gate, ffn_up, ffn_down, router, exp_gate, exp_up, exp_down, ln2_g, ln2_b):
    depth = w_in.shape[0]
    alpha = (2.0 * depth) ** 0.25
    for n_seq, seq_len in groups:
        assert seq_len % (NA_CHUNK_ROWS * GRID_W) == 0 and seq_len // WBLK >= 2
        assert (n_seq * seq_len) % ROW_TILE == 0 and seq_len % ROW_TILE == 0
        assert (n_seq * seq_len) % MOE_TILE == 0
    assert depth % 2 == 0, "the last layer must be a routed one: its combine kernel splits the outputs"
    row2 = lambda v: v.reshape(1, -1)

    win_tables = _window_tables(groups)
    nbr_tables = _nbr_tables(groups)
    cos_t, sin_t = _rope_tables(max(s for _, s in groups))
    _, _, pos_within, _ = _segment_tables(groups, ROW_TILE)
    bias_pairs = _nbr_bias_pairs(na_rpb)
    head_order = np.arange(H_A).reshape(HKV_A, GQA).T.reshape(-1)
    qa_cols = (head_order[:, None] * HEAD_DIM + np.arange(HEAD_DIM)[None, :]).reshape(-1)
    in_cols = np.concatenate([qa_cols, np.arange(QA_W, IN_W)])
    o_rows = np.concatenate([qa_cols, np.arange(QA_W, QA_W + B_W)])
    w_in_b = w_in[:, :, in_cols].astype(BF16)
    w_o_b = w_o[:, o_rows, :].astype(BF16)
    gnorm_a = gnorm_a[:, qa_cols]
    ffn_b = [w.astype(BF16) for w in (ffn_gate, ffn_up, ffn_down)]
    exp_b = [w.astype(BF16) for w in (exp_gate, exp_up, exp_down)]
    router_t = jnp.swapaxes(router, 1, 2)

    x = _embln(xa, xb, row2(emb_ln_g), row2(emb_ln_b))
    for l in range(depth):
        qa, ka, va, qb, kb, vb = _inproj(x, w_in_b[l], cos_t, sin_t, pos_within)
        oa = _window_attn(qa, ka, va, attn_sink[l], row2(gnorm_a[l]), win_tables)
        ob = _nbr_attn(qb, kb, vb, bias_pairs[l], row2(gnorm_b[l]), nbr_tables)
        x = _outproj(x, oa, ob, w_o_b[l, :QA_W], w_o_b[l, QA_W:], row2(ln1_g[l]), row2(ln1_b[l]), alpha)
        i = l // 2
        if l % 2 == 0:
            x = _ffn_dense(x, ffn_b[0][i:i + 1], ffn_b[1][i:i + 1], ffn_b[2][i:i + 1],
                           row2(ln2_g[l]), row2(ln2_b[l]), alpha)
        else:
            x = _moe_layer(x, router_t[i], exp_b[0][i], exp_b[1][i], exp_b[2][i],
                           row2(ln2_g[l]), row2(ln2_b[l]), alpha,
                           split_rows=xa.shape[0] if l == depth - 1 else None)
    return x


def kernel(x_prompt, x_sample, emb_ln_g, emb_ln_b, w_in, attn_sink, na_rpb, gnorm_a, gnorm_b, w_o, ln1_g, ln1_b,
           ffn_gate, ffn_up, ffn_down, router, exp_gate, exp_up, exp_down, ln2_g, ln2_b):
    groups = (x_prompt.shape[:2], x_sample.shape[:2])
    ta = x_prompt.shape[0] * x_prompt.shape[1]
    ya, yb = _trunk(x_prompt.reshape(ta, D_MODEL), x_sample.reshape(-1, D_MODEL), groups,
                    emb_ln_g, emb_ln_b, w_in, attn_sink, na_rpb, gnorm_a, gnorm_b, w_o, ln1_g, ln1_b,
                    ffn_gate, ffn_up, ffn_down, router, exp_gate, exp_up, exp_down, ln2_g, ln2_b)
    return ya.reshape(x_prompt.shape), yb.reshape(x_sample.shape)
```

```python
import functools

import numpy as np
import jax
import jax.numpy as jnp
from jax import lax
from jax.experimental import pallas as pl
from jax.experimental.pallas import tpu as pltpu

D_MODEL = 1024
HEAD_DIM = 64
H_A = 8
HKV_A = 2
GQA = H_A // HKV_A
H_B = 8
WINDOW = 128
WBLK = 128
ROPE_THETA = 10000.0
GRID_W = 64
NA_ROWS = 8
NA_COLS = 16
D_FF = 2816
N_EXPERTS = 8
TOP_K = 2
LN_EPS = 1e-5
NEG = -1e30
LOG2E = 1.4426950408889634
QA_W = H_A * HEAD_DIM
KVA_W = HKV_A * HEAD_DIM
B_W = H_B * HEAD_DIM
IN_W = QA_W + 2 * KVA_W + 3 * B_W
ROPE_W = QA_W + KVA_W

V7X_VMEM_LIMIT_BYTES = 56 * 1024 * 1024
LANES = 128
MXU_N = 256
ROW_TILE = 512
NA_CHUNK_ROWS = 8
MOE_TILE = 512
SLAB_ALIGN = 16
SLAB_ROWS = 1152

F32 = jnp.float32
BF16 = jnp.bfloat16


def _layernorm(y, g, b):
    mu = jnp.mean(y, axis=-1, keepdims=True)
    yc = y - mu
    var = jnp.mean(yc * yc, axis=-1, keepdims=True)
    return yc * lax.rsqrt(var + LN_EPS) * g + b


def _rmsnorm(o, g):
    ms = jnp.mean(o * o, axis=-1, keepdims=True)
    return o * lax.rsqrt(ms + LN_EPS) * g


def _dot_nt(a, b):
    return lax.dot_general(a, b, (((1,), (1,)), ((), ())), preferred_element_type=F32)


def _params(n_axes=1, vmem=None):
    return pltpu.CompilerParams(dimension_semantics=("arbitrary",) * n_axes, vmem_limit_bytes=vmem)


def _embln_kernel(xa_ref, xb_ref, g_ref, b_ref, o_ref, *, n_a):
    i = pl.program_id(0)

    @pl.when(i < n_a)
    def _():
        o_ref[...] = _layernorm(xa_ref[...], g_ref[...], b_ref[...])

    @pl.when(i >= n_a)
    def _():
        o_ref[...] = _layernorm(xb_ref[...], g_ref[...], b_ref[...])


def _embln(xa, xb, g, b):
    ta, tb = xa.shape[0], xb.shape[0]
    tm = ROW_TILE
    n_a, n_b = ta // tm, tb // tm
    return pl.pallas_call(
        functools.partial(_embln_kernel, n_a=n_a),
        grid=(n_a + n_b,),
        in_specs=[
            pl.BlockSpec((tm, D_MODEL), lambda i: (jnp.minimum(i, n_a - 1), 0)),
            pl.BlockSpec((tm, D_MODEL), lambda i: (jnp.maximum(i - n_a, 0), 0)),
            pl.BlockSpec((1, D_MODEL), lambda i: (0, 0)),
            pl.BlockSpec((1, D_MODEL), lambda i: (0, 0)),
        ],
        out_specs=pl.BlockSpec((tm, D_MODEL), lambda i: (i, 0)),
        out_shape=jax.ShapeDtypeStruct((ta + tb, D_MODEL), F32),
        compiler_params=_params(),
        name="emb_ln",
    )(xa, xb, g, b)


def _inproj_kernel(pos_ref, x_ref, w_ref, cos_ref, sin_ref,
                   qa_ref, ka_ref, va_ref, qb_ref, kb_ref, vb_ref):
    del pos_ref
    xb = x_ref[...].astype(BF16)
    cos = cos_ref[...]
    sin = sin_ref[...]
    lane = lax.broadcasted_iota(jnp.int32, cos.shape, 1)
    first_half = (lane % HEAD_DIM) < (HEAD_DIM // 2)
    scale = HEAD_DIM ** -0.5 * LOG2E

    def rope(h):
        partner = jnp.where(first_half, pltpu.roll(h, LANES - HEAD_DIM // 2, 1),
                            pltpu.roll(h, HEAD_DIM // 2, 1))
        return h * cos + partner * sin

    for c in range(QA_W // MXU_N):
        h = jnp.dot(xb, w_ref[:, c * MXU_N:(c + 1) * MXU_N], preferred_element_type=F32)
        for half in range(MXU_N // LANES):
            lo = c * MXU_N + half * LANES
            qa_ref[:, lo:lo + LANES] = (rope(h[:, half * LANES:(half + 1) * LANES]) * scale).astype(BF16)
    h = jnp.dot(xb, w_ref[:, QA_W:QA_W + 2 * KVA_W], preferred_element_type=F32)
    ka_ref[...] = rope(h[:, :KVA_W]).astype(BF16)
    va_ref[...] = h[:, KVA_W:].astype(BF16)
    off = QA_W + 2 * KVA_W
    for j, (ref, s) in enumerate(((qb_ref, scale), (kb_ref, 1.0), (vb_ref, 1.0))):
        for c in range(B_W // MXU_N):
            lo = off + j * B_W + c * MXU_N
            h = jnp.dot(xb, w_ref[:, lo:lo + MXU_N], preferred_element_type=F32)
            ref[:, c * MXU_N:(c + 1) * MXU_N] = (h * s).astype(BF16)


def _inproj(x, w_bf16, layer, cos_t, sin_t, pos_blk):
    t = x.shape[0]
    tm = ROW_TILE
    row = lambda i, p: (i, 0)
    out_shapes = [jax.ShapeDtypeStruct((t, w), BF16) for w in (QA_W, KVA_W, KVA_W, B_W, B_W, B_W)]
    return pl.pallas_call(
        _inproj_kernel,
        grid_spec=pltpu.PrefetchScalarGridSpec(
            num_scalar_prefetch=1,
            grid=(t // tm,),
            in_specs=[
                pl.BlockSpec((tm, D_MODEL), row),
                pl.BlockSpec((None, D_MODEL, IN_W), lambda i, p: (layer, 0, 0)),
                pl.BlockSpec((tm, LANES), lambda i, p: (p[i], 0)),
                pl.BlockSpec((tm, LANES), lambda i, p: (p[i], 0)),
            ],
            out_specs=[pl.BlockSpec((tm, s.shape[1]), row) for s in out_shapes],
        ),
        out_shape=out_shapes,
        compiler_params=_params(vmem=V7X_VMEM_LIMIT_BYTES),
        name="in_proj_rope",
    )(pos_blk, x, w_bf16, cos_t, sin_t)


def _split_heads(qc):
    lo = lax.broadcasted_iota(jnp.int32, qc.shape, 1) < HEAD_DIM
    zero = jnp.zeros_like(qc)
    return jnp.concatenate([jnp.where(lo, qc, zero), jnp.where(lo, zero, qc)], axis=0)


def _merge_heads(o2):
    m = o2.shape[0] // 2
    lo = lax.broadcasted_iota(jnp.int32, (m, LANES), 1) < HEAD_DIM
    return jnp.where(lo, o2[:m], o2[m:])


def _win_kernel(prev_ref, next_ref, first_ref, last_ref, sink_ref, q_ref, kp_ref, kc_ref, kn_ref,
                vp_ref, vc_ref, vn_ref, bias_ref, g_ref, o_ref, kwin, vwin):
    del prev_ref, next_ref
    i = pl.program_id(0)
    tq = q_ref.shape[0]
    nsub = tq // WBLK
    kwin[0:WBLK, :] = kp_ref[...]
    kwin[WBLK:WBLK + tq, :] = kc_ref[...]
    kwin[WBLK + tq:, :] = kn_ref[...]
    vwin[0:WBLK, 0:KVA_W] = vp_ref[...]
    vwin[WBLK:WBLK + tq, 0:KVA_W] = vc_ref[...]
    vwin[WBLK + tq:, 0:KVA_W] = vn_ref[...]
    vwin[:, KVA_W:] = jnp.ones((vwin.shape[0], LANES), BF16)
    first = first_ref[i]
    last = last_ref[i]
    g = g_ref[...]
    upper = lax.broadcasted_iota(jnp.int32, (2 * WBLK, 1), 0) < WBLK

    def body(j, carry):
        off = pl.multiple_of(j * WBLK, WBLK)
        bt = jnp.where((j == 0) & (first == 1), 0, jnp.where((j == nsub - 1) & (last == 1), 2, 1))
        bias = bias_ref[bt]
        bias2 = jnp.concatenate([bias, bias], axis=0)
        k3 = kwin[pl.ds(off, 3 * WBLK), :]
        v3 = vwin[pl.ds(off, 3 * WBLK), :]
        q = q_ref[pl.ds(off, WBLK), :]
        outs = []
        for c in range(QA_W // LANES):
            s = _dot_nt(_split_heads(q[:, c * LANES:(c + 1) * LANES]), k3) + bias2
            sk = jnp.where(upper, sink_ref[c] * LOG2E, sink_ref[GQA + c] * LOG2E)
            m = jnp.maximum(jnp.max(s, axis=-1, keepdims=True), sk)
            p = jnp.exp2(s - m).astype(BF16)
            of = jnp.dot(p, v3, preferred_element_type=F32)
            den = of[:, LANES:] + jnp.exp2(sk - m)
            outs.append(_merge_heads(of[:, :LANES] / den))
        o = jnp.concatenate(outs, axis=-1)
        o_ref[pl.ds(off, WBLK), :] = _rmsnorm(o, g).astype(BF16)
        return carry

    lax.fori_loop(0, nsub, body, 0, unroll=True)


def _window_attn(qa, ka, va, sink, gnorm, tables):
    t = qa.shape[0]
    tq = ROW_TILE
    prev_blk, next_blk, first, last, wbias = tables
    cur = pl.BlockSpec((tq, KVA_W), lambda i, pv, nx, fi, la: (i, 0))
    prv = pl.BlockSpec((WBLK, KVA_W), lambda i, pv, nx, fi, la: (pv[i], 0))
    nxt = pl.BlockSpec((WBLK, KVA_W), lambda i, pv, nx, fi, la: (nx[i], 0))
    return pl.pallas_call(
        _win_kernel,
        grid_spec=pltpu.PrefetchScalarGridSpec(
            num_scalar_prefetch=4,
            grid=(t // tq,),
            in_specs=[
                pl.BlockSpec(memory_space=pltpu.SMEM),
                pl.BlockSpec((tq, QA_W), lambda i, pv, nx, fi, la: (i, 0)),
                prv, cur, nxt, prv, cur, nxt,
                pl.BlockSpec(wbias.shape, lambda i, pv, nx, fi, la: (0, 0, 0)),
                pl.BlockSpec((1, QA_W), lambda i, pv, nx, fi, la: (0, 0)),
            ],
            out_specs=pl.BlockSpec((tq, QA_W), lambda i, pv, nx, fi, la: (i, 0)),
            scratch_shapes=[pltpu.VMEM((tq + 2 * WBLK, KVA_W), BF16),
                            pltpu.VMEM((tq + 2 * WBLK, KVA_W + LANES), BF16)],
        ),
        out_shape=jax.ShapeDtypeStruct((t, QA_W), BF16),
        compiler_params=_params(),
        name="window_attn",
    )(prev_blk, next_blk, first, last, sink, qa, ka, ka, ka, va, va, va, wbias, gnorm)


def _nbr_kernel(prev_ref, next_ref, row0_ref, rows_ref, q_ref, kp_ref, kc_ref, kn_ref,
                vp_ref, vc_ref, vn_ref, bias_ref, g_ref, o_ref, kwin, vwin):
    del prev_ref, next_ref
    j = pl.program_id(0)
    chunk = NA_CHUNK_ROWS * GRID_W
    ones = jnp.ones((3 * chunk, LANES), BF16)
    for c in range(B_W // LANES):
        vwin[:, (2 * c + 1) * LANES:(2 * c + 2) * LANES] = ones
    for w, (kr, vr) in enumerate(((kp_ref, vp_ref), (kc_ref, vc_ref), (kn_ref, vn_ref))):
        kwin[w * chunk:(w + 1) * chunk, :] = kr[...]
        for c in range(B_W // LANES):
            vwin[w * chunk:(w + 1) * chunk, 2 * c * LANES:(2 * c + 1) * LANES] = vr[:, c * LANES:(c + 1) * LANES]
    row0 = row0_ref[j]
    rows = rows_ref[j]
    g = g_ref[...]
    nkeys = NA_ROWS * GRID_W

    def body(rr, carry):
        r = row0 + rr
        rstart = jnp.clip(r - NA_ROWS // 2, 0, rows - NA_ROWS)
        bidx = (NA_ROWS - 1) - (r - rstart)
        loc = pl.multiple_of((rstart - row0 + NA_CHUNK_ROWS) * GRID_W, GRID_W)
        qoff = pl.multiple_of(rr * GRID_W, GRID_W)
        q = q_ref[pl.ds(qoff, GRID_W), :]
        outs = []
        for c in range(B_W // LANES):
            kc = kwin[pl.ds(loc, nkeys), c * LANES:(c + 1) * LANES]
            vc = vwin[pl.ds(loc, nkeys), 2 * c * LANES:(2 * c + 2) * LANES]
            s = _dot_nt(_split_heads(q[:, c * LANES:(c + 1) * LANES]), kc)
            s = s + jnp.concatenate([bias_ref[c, bidx + 2 * jj] for jj in range(NA_ROWS // 2)], axis=-1)
            m = jnp.max(s, axis=-1, keepdims=True)
            p = jnp.exp2(s - m).astype(BF16)
            of = jnp.dot(p, vc, preferred_element_type=F32)
            outs.append(_merge_heads(of[:, :LANES] / of[:, LANES:]))
        o = jnp.concatenate(outs, axis=-1)
        o_ref[pl.ds(qoff, GRID_W), :] = _rmsnorm(o, g).astype(BF16)
        return carry

    lax.fori_loop(0, NA_CHUNK_ROWS, body, 0, unroll=True)


def _nbr_attn(qb, kb, vb, bias_pairs, gnorm, tables):
    t = qb.shape[0]
    prev_c, next_c, row0, rows = tables
    chunk = NA_CHUNK_ROWS * GRID_W
    cur = pl.BlockSpec((chunk, B_W), lambda i, pv, nx, r0, rs: (i, 0))
    prv = pl.BlockSpec((chunk, B_W), lambda i, pv, nx, r0, rs: (pv[i], 0))
    nxt = pl.BlockSpec((chunk, B_W), lambda i, pv, nx, r0, rs: (nx[i], 0))
    return pl.pallas_call(
        _nbr_kernel,
        grid_spec=pltpu.PrefetchScalarGridSpec(
            num_scalar_prefetch=4,
            grid=(t // chunk,),
            in_specs=[
                cur, prv, cur, nxt, prv, cur, nxt,
                pl.BlockSpec(bias_pairs.shape, lambda i, pv, nx, r0, rs: (0, 0, 0, 0)),
                pl.BlockSpec((1, B_W), lambda i, pv, nx, r0, rs: (0, 0)),
            ],
            out_specs=cur,
            scratch_shapes=[pltpu.VMEM((3 * chunk, B_W), BF16), pltpu.VMEM((3 * chunk, 2 * B_W), BF16)],
        ),
        out_shape=jax.ShapeDtypeStruct((t, B_W), BF16),
        compiler_params=_params(vmem=V7X_VMEM_LIMIT_BYTES),
        name="nbr_attn",
    )(prev_c, next_c, row0, rows, qb, kb, kb, kb, vb, vb, vb, bias_pairs, gnorm)


def _route_tile(xn, rt_ref, w_ref, lp_ref, cnt_ref):
    tile = xn.shape[0]
    rt = rt_ref[...]
    rt_hi = rt.astype(BF16)
    rt_lo = (rt - rt_hi.astype(F32)).astype(BF16)
    xn_hi = xn.astype(BF16)
    xn_lo = (xn - xn_hi.astype(F32)).astype(BF16)
    part = _dot_nt(jnp.concatenate([rt_hi, rt_lo], axis=0), xn_hi)
    logits = part[:N_EXPERTS] + part[N_EXPERTS:] + _dot_nt(rt_hi, xn_lo)
    e = lax.broadcasted_iota(jnp.int32, logits.shape, 0)
    m1 = jnp.max(logits, axis=0, keepdims=True)
    i1 = jnp.min(jnp.where(logits == m1, e, N_EXPERTS), axis=0, keepdims=True)
    rest = jnp.where(e == i1, -jnp.inf, logits)
    m2 = jnp.max(rest, axis=0, keepdims=True)
    i2 = jnp.min(jnp.where(rest == m2, e, N_EXPERTS), axis=0, keepdims=True)
    t2 = jnp.exp(m2 - m1)
    w_ref[...] = jnp.concatenate([1.0 / (1.0 + t2), t2 / (1.0 + t2)], axis=0)
    oh1 = jnp.where(e == i1, 1.0, 0.0)
    oh2 = jnp.where(e == i2, 1.0, 0.0)
    both = oh1 + oh2
    earlier = jnp.where(lax.broadcasted_iota(jnp.int32, (tile, tile), 0)
                        < lax.broadcasted_iota(jnp.int32, (tile, tile), 1), 1.0, 0.0).astype(BF16)
    before = jnp.dot(both.astype(BF16), earlier, preferred_element_type=F32)
    count = jnp.sum(both, axis=1, keepdims=True).astype(jnp.int32)
    rows = ((count + (SLAB_ALIGN - 1)) // SLAB_ALIGN) * SLAB_ALIGN
    erow = lax.broadcasted_iota(jnp.int32, rows.shape, 0)
    start = jnp.zeros_like(rows)
    for k in range(N_EXPERTS - 1):
        start = start + jnp.where(erow > k, rows[k:k + 1, :], 0)
    where = start.astype(F32) + before
    lp_ref[...] = jnp.concatenate([jnp.sum(oh1 * where, axis=0, keepdims=True),
                                   jnp.sum(oh2 * where, axis=0, keepdims=True)], axis=0).astype(jnp.int32)
    cnt_ref[0] = jnp.broadcast_to(rows, cnt_ref.shape[1:])


def _outproj_kernel(x_ref, oa_ref, ob_ref, wa_ref, wb_ref, g_ref, b_ref, *rest, alpha, route):
    mix = jnp.dot(oa_ref[...], wa_ref[...], preferred_element_type=F32)
    mix = mix + jnp.dot(ob_ref[...], wb_ref[...], preferred_element_type=F32)
    xn = _layernorm(alpha * x_ref[...] + mix, g_ref[...], b_ref[...])
    if route:
        rt_ref, o_ref, w_ref, lp_ref, cnt_ref = rest
        _route_tile(xn, rt_ref, w_ref, lp_ref, cnt_ref)
    else:
        o_ref, = rest
    o_ref[...] = xn


def _outproj(x, oa, ob, w_o, layer, g, b, alpha, router_t=None):
    t = x.shape[0]
    tm = ROW_TILE
    assert tm == MOE_TILE
    row = lambda i: (i, 0)
    const = lambda i: (0, 0)
    in_specs = [
        pl.BlockSpec((tm, D_MODEL), row),
        pl.BlockSpec((tm, QA_W), row),
        pl.BlockSpec((tm, B_W), row),
        pl.BlockSpec((None, QA_W, D_MODEL), lambda i: (layer, 0, 0)),
        pl.BlockSpec((None, B_W, D_MODEL), lambda i: (layer, QA_W // B_W, 0)),
        pl.BlockSpec((1, D_MODEL), const),
        pl.BlockSpec((1, D_MODEL), const),
    ]
    out_specs = [pl.BlockSpec((tm, D_MODEL), row)]
    out_shape = [jax.ShapeDtypeStruct((t, D_MODEL), F32)]
    args = [x, oa, ob, w_o, w_o, g, b]
    if router_t is not None:
        in_specs.append(pl.BlockSpec((N_EXPERTS, D_MODEL), const))
        args.append(router_t)
        out_specs += [pl.BlockSpec((TOP_K, tm), lambda i: (0, i)), pl.BlockSpec((TOP_K, tm), lambda i: (0, i)),
                      pl.BlockSpec((1, N_EXPERTS, LANES), lambda i: (i, 0, 0))]
        out_shape += [jax.ShapeDtypeStruct((TOP_K, t), F32), jax.ShapeDtypeStruct((TOP_K, t), jnp.int32),
                      jax.ShapeDtypeStruct((t // tm, N_EXPERTS, LANES), jnp.int32)]
    return pl.pallas_call(
        functools.partial(_outproj_kernel, alpha=alpha, route=router_t is not None),
        grid=(t // tm,),
        in_specs=in_specs,
        out_specs=out_specs,
        out_shape=out_shape,
        compiler_params=_params(vmem=V7X_VMEM_LIMIT_BYTES),
        name="out_proj_ln1",
    )(*args)


def _swiglu_tile(xb, wg_ref, wu_ref, wd_ref, h_ref):
    for c in range(D_FF // MXU_N):
        sl = slice(c * MXU_N, (c + 1) * MXU_N)
        gate = jnp.dot(xb, wg_ref[0, :, sl], preferred_element_type=F32)
        up = jnp.dot(xb, wu_ref[0, :, sl], preferred_element_type=F32)
        h_ref[:, sl] = (gate / (1.0 + jnp.exp(-gate)) * up).astype(BF16)
    return jnp.dot(h_ref[...], wd_ref[0], preferred_element_type=F32)


def _ffn_kernel(x_ref, wg_ref, wu_ref, wd_ref, g_ref, b_ref, o_ref, h_ref, *, alpha):
    x = x_ref[...]
    f = _swiglu_tile(x.astype(BF16), wg_ref, wu_ref, wd_ref, h_ref)
    o_ref[...] = _layernorm(alpha * x + f, g_ref[...], b_ref[...])


def _ffn_dense(x, wg, wu, wd, layer, g, b, alpha):
    t = x.shape[0]
    tm = ROW_TILE
    row = lambda i: (i, 0)
    const = lambda i: (0, 0)
    const3 = lambda i: (layer, 0, 0)
    return pl.pallas_call(
        functools.partial(_ffn_kernel, alpha=alpha),
        grid=(t // tm,),
        in_specs=[
            pl.BlockSpec((tm, D_MODEL), row),
            pl.BlockSpec((1, D_MODEL, D_FF), const3),
            pl.BlockSpec((1, D_MODEL, D_FF), const3),
            pl.BlockSpec((1, D_FF, D_MODEL), const3),
            pl.BlockSpec((1, D_MODEL), const),
            pl.BlockSpec((1, D_MODEL), const),
        ],
        out_specs=pl.BlockSpec((tm, D_MODEL), row),
        out_shape=jax.ShapeDtypeStruct((t, D_MODEL), F32),
        scratch_shapes=[pltpu.VMEM((tm, D_FF), BF16)],
        compiler_params=_params(vmem=V7X_VMEM_LIMIT_BYTES),
        name="ffn_dense_ln2",
    )(x, wg, wu, wd, g, b)


def _slab_copies(n_rows, src, src_row, dst, dst_row, sem, action, max_rows):
    shift = SLAB_ALIGN.bit_length() - 1
    for b in reversed(range((max_rows // SLAB_ALIGN).bit_length())):
        size = SLAB_ALIGN << b
        if size > max_rows:
            continue
        done = (n_rows >> (shift + b + 1)) << (shift + b + 1)

        @pl.when(((n_rows >> (shift + b)) & 1) == 1)
        def _(size=size, done=done):
            s = pl.multiple_of(src_row + done, SLAB_ALIGN)
            d = pl.multiple_of(dst_row + done, SLAB_ALIGN)
            action(pltpu.make_async_copy(src.at[pl.ds(s, size)], dst.at[pl.ds(d, size)], sem))


def _start(copy):
    copy.start()


def _wait(copy):
    copy.wait()


def _dispatch_kernel(n_ref, off_ref, loff_ref, padoff_ref, padlen_ref, nvalid_ref, x_ref, lp_ref, xs_hbm,
                     slab, zeros, sems):
    i = pl.program_id(0)
    sem = sems.at[0]

    @pl.when(i == 0)
    def _():
        zeros[...] = jnp.zeros_like(zeros)
        tile_rows = zeros.shape[0]
        for action in (_start, _wait):
            for e in range(N_EXPERTS):
                _slab_copies(padlen_ref[e], zeros, 0, xs_hbm, padoff_ref[e], sem, action, tile_rows)

            def unused_tile(j, carry, action=action):
                row = pl.multiple_of(j * tile_rows, tile_rows)
                action(pltpu.make_async_copy(zeros, xs_hbm.at[pl.ds(row, tile_rows)], sem))
                return carry
            lax.fori_loop(nvalid_ref[0], xs_hbm.shape[0] // tile_rows, unused_tile, 0)

    lp = lp_ref[...]
    q = lax.broadcasted_iota(jnp.int32, (slab.shape[0], lp.shape[1]), 0)
    sel = jnp.where(q == lp[0:1, :], 1.0, jnp.where(q == lp[1:2, :], 1.0, 0.0))
    slab[...] = jnp.dot(sel.astype(BF16), x_ref[...].astype(BF16), preferred_element_type=F32).astype(BF16)
    for action in (_start, _wait):
        for e in range(N_EXPERTS):
            k = i * N_EXPERTS + e
            _slab_copies(n_ref[k], slab, loff_ref[k], xs_hbm, off_ref[k], sem, action, MOE_TILE)


def _moe_dispatch(x, lp, route, n_rows):
    t = x.shape[0]
    tt = MOE_TILE
    imap2 = lambda i, *_: (i, 0)
    return pl.pallas_call(
        _dispatch_kernel,
        grid_spec=pltpu.PrefetchScalarGridSpec(
            num_scalar_prefetch=6,
            grid=(t // tt,),
            in_specs=[
                pl.BlockSpec((tt, D_MODEL), imap2),
                pl.BlockSpec((TOP_K, tt), lambda i, *_: (0, i)),
            ],
            out_specs=pl.BlockSpec(memory_space=pl.ANY),
            scratch_shapes=[
                pltpu.VMEM((SLAB_ROWS, D_MODEL), BF16),
                pltpu.VMEM((ROW_TILE, D_MODEL), BF16),
                pltpu.SemaphoreType.DMA((1,)),
            ],
        ),
        out_shape=jax.ShapeDtypeStruct((n_rows, D_MODEL), BF16),
        compiler_params=_params(vmem=V7X_VMEM_LIMIT_BYTES),
        name="moe_dispatch",
    )(route["n"], route["off"], route["loff"], route["pad_off"], route["pad_len"], route["nvalid"], x,
      lp)


def _moe_kernel(tile_e_ref, nvalid_ref, x_ref, wg_ref, wu_ref, wd_ref, y_ref, h_ref):
    del tile_e_ref
    j = pl.program_id(0)
    nv = nvalid_ref[0]

    @pl.when(j < nv)
    def _():
        y_ref[...] = _swiglu_tile(x_ref[...], wg_ref, wu_ref, wd_ref, h_ref).astype(BF16)

    @pl.when(j >= nv)
    def _():
        y_ref[...] = jnp.zeros_like(y_ref)


def _moe_experts(xs, tile_e, nvalid, wg, wu, wd, layer):
    tm = ROW_TILE
    n_tiles = xs.shape[0] // tm
    wspec = lambda shape: pl.BlockSpec((None, 1) + shape, lambda j, te, nv: (layer, te[j], 0, 0))
    return pl.pallas_call(
        _moe_kernel,
        grid_spec=pltpu.PrefetchScalarGridSpec(
            num_scalar_prefetch=2,
            grid=(n_tiles,),
            in_specs=[
                pl.BlockSpec((tm, D_MODEL), lambda j, te, nv: (jnp.minimum(j, nv[0] - 1), 0)),
                wspec((D_MODEL, D_FF)), wspec((D_MODEL, D_FF)), wspec((D_FF, D_MODEL)),
            ],
            out_specs=pl.BlockSpec((tm, D_MODEL), lambda j, te, nv: (j, 0)),
            scratch_shapes=[pltpu.VMEM((tm, D_FF), BF16)],
        ),
        out_shape=jax.ShapeDtypeStruct(xs.shape, BF16),
        compiler_params=_params(vmem=V7X_VMEM_LIMIT_BYTES),
        name="moe_experts",
    )(tile_e, nvalid, xs, wg, wu, wd)


def _combine_kernel(n_ref, off_ref, loff_ref, x_ref, lp_ref, w_ref, ys_hbm, g_ref, b_ref, *rest, alpha, split):
    out_refs, (yslab, sems) = rest[:-2], rest[-2:]
    i = pl.program_id(0)
    slot = i % 2

    def copies(tile, sl, action):
        for e in range(N_EXPERTS):
            k = tile * N_EXPERTS + e
            _slab_copies(n_ref[k], ys_hbm, off_ref[k], yslab.at[sl], loff_ref[k], sems.at[sl], action, MOE_TILE)

    @pl.when(i == 0)
    def _():
        yslab[...] = jnp.zeros_like(yslab)
        copies(0, 0, _start)

    @pl.when(i + 1 < pl.num_programs(0))
    def _():
        copies(i + 1, 1 - slot, _start)

    copies(i, slot, _wait)
    lp = lp_ref[...]
    w = w_ref[...]
    q = lax.broadcasted_iota(jnp.int32, (lp.shape[0], yslab.shape[1]), 1)
    gate = jnp.where(q == lp[:, 0:1], w[:, 0:1], 0.0) + jnp.where(q == lp[:, 1:2], w[:, 1:2], 0.0)
    f = jnp.dot(gate.astype(BF16), yslab[slot], preferred_element_type=F32)
    o = _layernorm(alpha * x_ref[...] + f, g_ref[...], b_ref[...])
    if split is None:
        out_refs[0][...] = o
    else:
        @pl.when(i < split)
        def _():
            out_refs[0][...] = o

        @pl.when(i >= split)
        def _():
            out_refs[1][...] = o


def _moe_combine(x, gate_w, lp_cols, route, y_sorted, g, b, alpha, split_rows=None):
    t = x.shape[0]
    tt = MOE_TILE
    n = t // tt
    imap2 = lambda i, *_: (i, 0)
    const = lambda i, *_: (0, 0)
    if split_rows is None:
        split = None
        out_specs = pl.BlockSpec((tt, D_MODEL), imap2)
        out_shape = jax.ShapeDtypeStruct((t, D_MODEL), F32)
    else:
        assert split_rows % tt == 0 and 0 < split_rows < t
        split = split_rows // tt
        out_specs = [pl.BlockSpec((tt, D_MODEL), lambda i, *_: (jnp.minimum(i, split - 1), 0)),
                     pl.BlockSpec((tt, D_MODEL), lambda i, *_: (jnp.maximum(i - split, 0), 0))]
        out_shape = [jax.ShapeDtypeStruct((split_rows, D_MODEL), F32),
                     jax.ShapeDtypeStruct((t - split_rows, D_MODEL), F32)]
    return pl.pallas_call(
        functools.partial(_combine_kernel, alpha=alpha, split=split),
        grid_spec=pltpu.PrefetchScalarGridSpec(
            num_scalar_prefetch=3,
            grid=(n,),
            in_specs=[
                pl.BlockSpec((tt, D_MODEL), imap2),
                pl.BlockSpec((tt, TOP_K), imap2),
                pl.BlockSpec((tt, TOP_K), imap2),
                pl.BlockSpec(memory_space=pl.ANY),
                pl.BlockSpec((1, D_MODEL), const),
                pl.BlockSpec((1, D_MODEL), const),
            ],
            out_specs=out_specs,
            scratch_shapes=[pltpu.VMEM((2, SLAB_ROWS, D_MODEL), BF16), pltpu.SemaphoreType.DMA((2,))],
        ),
        out_shape=out_shape,
        compiler_params=_params(vmem=V7X_VMEM_LIMIT_BYTES),
        name="moe_combine_ln2",
    )(route["n"], route["off"], route["loff"], x, lp_cols, gate_w, y_sorted, g, b)


def _route(n, t):
    i32 = jnp.int32
    loff = jnp.cumsum(n, axis=1) - n
    seg_rows = jnp.sum(n, axis=0)
    seg_len = ((seg_rows + ROW_TILE - 1) // ROW_TILE) * ROW_TILE
    ends = jnp.cumsum(seg_len)
    goff = ends - seg_len
    off = goff[None, :] + jnp.cumsum(n, axis=0) - n
    n_tiles = _sorted_rows(t) // ROW_TILE
    nvalid = ends[-1] // ROW_TILE
    starts = jnp.arange(n_tiles, dtype=i32) * ROW_TILE
    tile_e = jnp.minimum(jnp.sum((starts[:, None] >= ends[None, :]).astype(i32), axis=1), N_EXPERTS - 1)
    tile_e = jnp.where(jnp.arange(n_tiles) < nvalid, tile_e, tile_e[nvalid - 1])
    flat = lambda v: v.reshape(-1).astype(i32)
    return dict(n=flat(n), off=flat(off), loff=flat(loff), pad_off=flat(goff + seg_rows),
                pad_len=flat(seg_len - seg_rows), tile_e=flat(tile_e), nvalid=flat(nvalid))


def _sorted_rows(t):
    worst = t * TOP_K + (t // MOE_TILE) * N_EXPERTS * (SLAB_ALIGN - 1) + N_EXPERTS * (ROW_TILE - 1)
    return -(-worst // ROW_TILE) * ROW_TILE


def _moe_layer(x, gate_w, lp, slab_rows, wg, wu, wd, layer, g, b, alpha, split_rows=None):
    t = x.shape[0]
    route = _route(slab_rows[:, :, 0], t)
    xs = _moe_dispatch(x, lp, route, _sorted_rows(t))
    ys = _moe_experts(xs, route["tile_e"], route["nvalid"], wg, wu, wd, layer)
    return _moe_combine(x, gate_w.T, lp.T, route, ys, g, b, alpha, split_rows)


def _segment_tables(groups, unit):
    prev, nxt, within, per_seq = [], [], [], []
    base = 0
    for n_seq, seq_len in groups:
        nb = seq_len // unit
        for _ in range(n_seq):
            for n in range(nb):
                i = base + n
                prev.append(i - 1 if n > 0 else i)
                nxt.append(i + 1 if n < nb - 1 else i)
                within.append(n)
                per_seq.append(nb)
            base += nb
    as_i32 = lambda v: jnp.asarray(np.asarray(v, np.int32))
    return as_i32(prev), as_i32(nxt), as_i32(within), as_i32(per_seq)


def _window_tables(groups):
    prev, nxt, within, per_seq = _segment_tables(groups, ROW_TILE)
    per_tile = ROW_TILE // WBLK
    tile = jnp.arange(prev.shape[0], dtype=jnp.int32)
    first = (within == 0).astype(jnp.int32)
    last = (within == per_seq - 1).astype(jnp.int32)
    prev_blk = jnp.where(first == 1, tile * per_tile, tile * per_tile - 1)
    next_blk = jnp.where(last == 1, tile * per_tile + per_tile - 1, tile * per_tile + per_tile)
    i = np.arange(WBLK)[:, None]
    jj = np.arange(3 * WBLK)[None, :]
    band = np.abs(jj - WBLK - i) <= WINDOW
    masks = []
    for no_prev, no_next in ((True, False), (False, False), (False, True)):
        inr = np.ones_like(band)
        if no_prev:
            inr = inr & (jj >= WBLK)
        if no_next:
            inr = inr & (jj < 2 * WBLK)
        masks.append(np.where(band & inr, 0.0, NEG))
    return prev_blk, next_blk, first, last, jnp.asarray(np.stack(masks).astype(np.float32))


def _nbr_tables(groups):
    chunk = NA_CHUNK_ROWS * GRID_W
    prev, nxt, within, per_seq = _segment_tables(groups, chunk)
    return prev, nxt, within * NA_CHUNK_ROWS, per_seq * NA_CHUNK_ROWS


def _nbr_bias_pairs(na_rpb):
    c = np.arange(GRID_W)[:, None]
    kc = np.arange(GRID_W)[None, :]
    cstart = np.clip(c - NA_COLS // 2, 0, GRID_W - NA_COLS)
    allowed = (kc >= cstart) & (kc < cstart + NA_COLS)
    dc = np.clip(kc - c + (NA_COLS - 1), 0, 2 * NA_COLS - 2)
    full = jnp.where(jnp.asarray(allowed)[None, None, None], na_rpb.astype(F32)[:, :, :, dc] * LOG2E, NEG)
    pairs = jnp.concatenate([full[:, :, :-1], full[:, :, 1:]], axis=-1)
    nl, nh, nd = pairs.shape[:3]
    pairs = pairs.reshape(nl, nh // 2, 2, nd, GRID_W, 2 * GRID_W).transpose(0, 1, 3, 2, 4, 5)
    return pairs.reshape(nl, nh // 2, nd, 2 * GRID_W, 2 * GRID_W)


def _rope_tables(max_len):
    half = HEAD_DIM // 2
    inv = 1.0 / (ROPE_THETA ** (jnp.arange(half, dtype=F32) / half))
    ang = jnp.arange(max_len, dtype=F32)[:, None] * inv[None, :]
    cos, sin = jnp.cos(ang), jnp.sin(ang)
    reps = LANES // HEAD_DIM
    return jnp.tile(jnp.concatenate([cos, cos], axis=-1), (1, reps)), \
        jnp.tile(jnp.concatenate([-sin, sin], axis=-1), (1, reps))


def _trunk(xa, xb, groups, emb_ln_g, emb_ln_b, w_in, attn_sink, na_rpb, gnorm_a, gnorm_b, w_o, ln1_g, ln1_b,
           ffn_gate, ffn_up, ffn_down, router, exp_gate, exp_up, exp_down, ln2_g, ln2_b):
    depth = w_in.shape[0]
    alpha = (2.0 * depth) ** 0.25
    for n_seq, seq_len in groups:
        assert seq_len % (NA_CHUNK_ROWS * GRID_W) == 0 and seq_len // WBLK >= 2
        assert (n_seq * seq_len) % ROW_TILE == 0 and seq_len % ROW_TILE == 0
        assert (n_seq * seq_len) % MOE_TILE == 0
    assert depth % 2 == 0, "the last layer must be a routed one: its combine kernel splits the outputs"
    row2 = lambda v: v.reshape(1, -1)

    win_tables = _window_tables(groups)
    nbr_tables = _nbr_tables(groups)
    cos_t, sin_t = _rope_tables(max(s for _, s in groups))
    _, _, pos_within, _ = _segment_tables(groups, ROW_TILE)
    bias_pairs = _nbr_bias_pairs(na_rpb)
    def pair_heads(w, axis):
        shape = w.shape[:axis] + (HKV_A, GQA, HEAD_DIM) + w.shape[axis + 1:]
        return jnp.swapaxes(w.reshape(shape), axis, axis + 1).reshape(w.shape)

    w_in_b = jnp.concatenate([pair_heads(w_in[:, :, :QA_W], 2), w_in[:, :, QA_W:]], axis=2).astype(BF16)
    w_o_b = jnp.concatenate([pair_heads(w_o[:, :QA_W], 1), w_o[:, QA_W:]], axis=1).astype(BF16)
    gnorm_a = pair_heads(gnorm_a, 1)
    ffn_b = [w.astype(BF16) for w in (ffn_gate, ffn_up, ffn_down)]
    exp_b = [w.astype(BF16) for w in (exp_gate, exp_up, exp_down)]
    router_t = jnp.swapaxes(router, 1, 2)

    x = _embln(xa, xb, row2(emb_ln_g), row2(emb_ln_b))
    for l in range(depth):
        qa, ka, va, qb, kb, vb = _inproj(x, w_in_b, l, cos_t, sin_t, pos_within)
        oa = _window_attn(qa, ka, va, attn_sink[l], row2(gnorm_a[l]), win_tables)
        ob = _nbr_attn(qb, kb, vb, bias_pairs[l], row2(gnorm_b[l]), nbr_tables)
        i = l // 2
        if l % 2 == 0:
            x, = _outproj(x, oa, ob, w_o_b, l, row2(ln1_g[l]), row2(ln1_b[l]), alpha)
            x = _ffn_dense(x, *ffn_b, i, row2(ln2_g[l]), row2(ln2_b[l]), alpha)
        else:
            x, gate_w, lp, slab_rows = _outproj(x, oa, ob, w_o_b, l, row2(ln1_g[l]), row2(ln1_b[l]), alpha,
                                                router_t=router_t[i])
            x = _moe_layer(x, gate_w, lp, slab_rows, *exp_b, i, row2(ln2_g[l]), row2(ln2_b[l]), alpha,
                           split_rows=xa.shape[0] if l == depth - 1 else None)
    return x


def kernel(x_prompt, x_sample, emb_ln_g, emb_ln_b, w_in, attn_sink, na_rpb, gnorm_a, gnorm_b, w_o, ln1_g, ln1_b,
           ffn_gate, ffn_up, ffn_down, router, exp_gate, exp_up, exp_down, ln2_g, ln2_b):
    groups = (x_prompt.shape[:2], x_sample.shape[:2])
    ta = x_prompt.shape[0] * x_prompt.shape[1]
    ya, yb = _trunk(x_prompt.reshape(ta, D_MODEL), x_sample.reshape(-1, D_MODEL), groups,
                    emb_ln_g, emb_ln_b, w_in, attn_sink, na_rpb, gnorm_a, gnorm_b, w_o, ln1_g, ln1_b,
                    ffn_gate, ffn_up, ffn_down, router, exp_gate, exp_up, exp_down, ln2_g, ln2_b)
    return ya.reshape(x_prompt.shape), yb.reshape(x_sample.shape)
```

```python
import functools

import numpy as np
import jax
import jax.numpy as jnp
from jax import lax
from jax.experimental import pallas as pl
from jax.experimental.pallas import tpu as pltpu

D_MODEL = 1024
HEAD_DIM = 64
H_A = 8
HKV_A = 2
GQA = H_A // HKV_A
H_B = 8
WINDOW = 128
WBLK = 128
ROPE_THETA = 10000.0
GRID_W = 64
NA_ROWS = 8
NA_COLS = 16
D_FF = 2816
N_EXPERTS = 8
TOP_K = 2
LN_EPS = 1e-5
NEG = -1e30
LOG2E = 1.4426950408889634
QA_W = H_A * HEAD_DIM
KVA_W = HKV_A * HEAD_DIM
B_W = H_B * HEAD_DIM
IN_W = QA_W + 2 * KVA_W + 3 * B_W
ROPE_W = QA_W + KVA_W

V7X_VMEM_LIMIT_BYTES = 56 * 1024 * 1024
LANES = 128
MXU_N = 256
ROW_TILE = 512
NA_CHUNK_ROWS = 8
MOE_TILE = 512
SLAB_ALIGN = 16
SLAB_ROWS = 1152

F32 = jnp.float32
BF16 = jnp.bfloat16


def _layernorm(y, g, b):
    mu = jnp.mean(y, axis=-1, keepdims=True)
    yc = y - mu
    var = jnp.mean(yc * yc, axis=-1, keepdims=True)
    return yc * lax.rsqrt(var + LN_EPS) * g + b


def _rmsnorm(o, g):
    ms = jnp.mean(o * o, axis=-1, keepdims=True)
    return o * lax.rsqrt(ms + LN_EPS) * g


def _dot_nt(a, b):
    return lax.dot_general(a, b, (((1,), (1,)), ((), ())), preferred_element_type=F32)


def _params(n_axes=1, vmem=None):
    return pltpu.CompilerParams(dimension_semantics=("arbitrary",) * n_axes, vmem_limit_bytes=vmem)


def _inproj_kernel(pos_ref, *refs, embed_tiles):
    del pos_ref
    if embed_tiles is None:
        x_ref, w_ref, cos_ref, sin_ref, qa_ref, ka_ref, va_ref, qb_ref, kb_ref, vb_ref = refs
        x = x_ref[...]
    else:
        (xa_ref, xb_ref, eg_ref, eb_ref, w_ref, cos_ref, sin_ref,
         qa_ref, ka_ref, va_ref, qb_ref, kb_ref, vb_ref, xln_ref) = refs
        raw = jnp.where(pl.program_id(0) < embed_tiles, xa_ref[...], xb_ref[...])
        x = _layernorm(raw, eg_ref[...], eb_ref[...])
        xln_ref[...] = x
    xb = x.astype(BF16)
    cos = cos_ref[...]
    sin = sin_ref[...]
    lane = lax.broadcasted_iota(jnp.int32, cos.shape, 1)
    first_half = (lane % HEAD_DIM) < (HEAD_DIM // 2)
    scale = HEAD_DIM ** -0.5 * LOG2E

    def rope(h):
        partner = jnp.where(first_half, pltpu.roll(h, LANES - HEAD_DIM // 2, 1),
                            pltpu.roll(h, HEAD_DIM // 2, 1))
        return h * cos + partner * sin

    for c in range(QA_W // MXU_N):
        h = jnp.dot(xb, w_ref[:, c * MXU_N:(c + 1) * MXU_N], preferred_element_type=F32)
        for half in range(MXU_N // LANES):
            lo = c * MXU_N + half * LANES
            qa_ref[:, lo:lo + LANES] = (rope(h[:, half * LANES:(half + 1) * LANES]) * scale).astype(BF16)
    h = jnp.dot(xb, w_ref[:, QA_W:QA_W + 2 * KVA_W], preferred_element_type=F32)
    ka_ref[...] = rope(h[:, :KVA_W]).astype(BF16)
    va_ref[...] = h[:, KVA_W:].astype(BF16)
    off = QA_W + 2 * KVA_W
    for j, (ref, s) in enumerate(((qb_ref, scale), (kb_ref, 1.0), (vb_ref, 1.0))):
        for c in range(B_W // MXU_N):
            lo = off + j * B_W + c * MXU_N
            h = jnp.dot(xb, w_ref[:, lo:lo + MXU_N], preferred_element_type=F32)
            ref[:, c * MXU_N:(c + 1) * MXU_N] = (h * s).astype(BF16)


def _inproj(x, w_bf16, layer, cos_t, sin_t, pos_blk, embed=None):
    tm = ROW_TILE
    row = lambda i, p: (i, 0)
    const = lambda i, p: (0, 0)
    if embed is None:
        t = x.shape[0]
        embed_tiles = None
        x_specs = [pl.BlockSpec((tm, D_MODEL), row)]
        x_args = [x]
    else:
        xb, eg, eb = embed
        t = x.shape[0] + xb.shape[0]
        embed_tiles = x.shape[0] // tm
        x_specs = [pl.BlockSpec((tm, D_MODEL), lambda i, p: (jnp.minimum(i, embed_tiles - 1), 0)),
                   pl.BlockSpec((tm, D_MODEL), lambda i, p: (jnp.maximum(i - embed_tiles, 0), 0)),
                   pl.BlockSpec((1, D_MODEL), const), pl.BlockSpec((1, D_MODEL), const)]
        x_args = [x, xb, eg, eb]
    out_shapes = [jax.ShapeDtypeStruct((t, w), BF16) for w in (QA_W, KVA_W, KVA_W, B_W, B_W, B_W)]
    if embed is not None:
        out_shapes.append(jax.ShapeDtypeStruct((t, D_MODEL), F32))
    return pl.pallas_call(
        functools.partial(_inproj_kernel, embed_tiles=embed_tiles),
        grid_spec=pltpu.PrefetchScalarGridSpec(
            num_scalar_prefetch=1,
            grid=(t // tm,),
            in_specs=x_specs + [
                pl.BlockSpec((None, D_MODEL, IN_W), lambda i, p: (layer, 0, 0)),
                pl.BlockSpec((tm, LANES), lambda i, p: (p[i], 0)),
                pl.BlockSpec((tm, LANES), lambda i, p: (p[i], 0)),
            ],
            out_specs=[pl.BlockSpec((tm, s.shape[1]), row) for s in out_shapes],
        ),
        out_shape=out_shapes,
        compiler_params=_params(vmem=V7X_VMEM_LIMIT_BYTES),
        name="in_proj_rope",
    )(pos_blk, *x_args, w_bf16, cos_t, sin_t)


def _split_heads(qc):
    lo = lax.broadcasted_iota(jnp.int32, qc.shape, 1) < HEAD_DIM
    zero = jnp.zeros_like(qc)
    return jnp.concatenate([jnp.where(lo, qc, zero), jnp.where(lo, zero, qc)], axis=0)


def _merge_heads(o2):
    m = o2.shape[0] // 2
    lo = lax.broadcasted_iota(jnp.int32, (m, LANES), 1) < HEAD_DIM
    return jnp.where(lo, o2[:m], o2[m:])


def _win_kernel(prev_ref, next_ref, first_ref, last_ref, sink_ref, q_ref, kp_ref, kc_ref, kn_ref,
                vp_ref, vc_ref, vn_ref, bias_ref, g_ref, o_ref, kwin, vwin):
    del prev_ref, next_ref
    i = pl.program_id(0)
    tq = q_ref.shape[0]
    nsub = tq // WBLK
    kwin[0:WBLK, :] = kp_ref[...]
    kwin[WBLK:WBLK + tq, :] = kc_ref[...]
    kwin[WBLK + tq:, :] = kn_ref[...]
    vwin[0:WBLK, 0:KVA_W] = vp_ref[...]
    vwin[WBLK:WBLK + tq, 0:KVA_W] = vc_ref[...]
    vwin[WBLK + tq:, 0:KVA_W] = vn_ref[...]
    vwin[:, KVA_W:] = jnp.ones((vwin.shape[0], LANES), BF16)
    first = first_ref[i]
    last = last_ref[i]
    g = g_ref[...]

    def body(j, carry):
        off = pl.multiple_of(j * WBLK, WBLK)
        bt = jnp.where((j == 0) & (first == 1), 0, jnp.where((j == nsub - 1) & (last == 1), 2, 1))
        bias = bias_ref[bt]
        k3 = kwin[pl.ds(off, 3 * WBLK), :]
        v3 = vwin[pl.ds(off, 3 * WBLK), :]
        q = q_ref[pl.ds(off, WBLK), :]
        outs = []
        for c in range(QA_W // LANES):
            qc = q[:, c * LANES:(c + 1) * LANES]
            lo = lax.broadcasted_iota(jnp.int32, qc.shape, 1) < HEAD_DIM
            halves = []
            for hh, qh in ((c, jnp.where(lo, qc, jnp.zeros_like(qc))), (GQA + c, jnp.where(lo, jnp.zeros_like(qc), qc))):
                s = _dot_nt(qh, k3) + bias
                sk = sink_ref[hh] * LOG2E
                m = jnp.maximum(jnp.max(s, axis=-1, keepdims=True), sk)
                p = jnp.exp2(s - m).astype(BF16)
                of = jnp.dot(p, v3, preferred_element_type=F32)
                halves.append(of[:, :LANES] / (of[:, LANES:] + jnp.exp2(sk - m)))
            outs.append(jnp.where(lo, halves[0], halves[1]))
        o = jnp.concatenate(outs, axis=-1)
        o_ref[pl.ds(off, WBLK), :] = _rmsnorm(o, g).astype(BF16)
        return carry

    lax.fori_loop(0, nsub, body, 0, unroll=True)


def _window_attn(qa, ka, va, sink, gnorm, tables):
    t = qa.shape[0]
    tq = ROW_TILE
    prev_blk, next_blk, first, last, wbias = tables
    cur = pl.BlockSpec((tq, KVA_W), lambda i, pv, nx, fi, la: (i, 0))
    prv = pl.BlockSpec((WBLK, KVA_W), lambda i, pv, nx, fi, la: (pv[i], 0))
    nxt = pl.BlockSpec((WBLK, KVA_W), lambda i, pv, nx, fi, la: (nx[i], 0))
    return pl.pallas_call(
        _win_kernel,
        grid_spec=pltpu.PrefetchScalarGridSpec(
            num_scalar_prefetch=4,
            grid=(t // tq,),
            in_specs=[
                pl.BlockSpec(memory_space=pltpu.SMEM),
                pl.BlockSpec((tq, QA_W), lambda i, pv, nx, fi, la: (i, 0)),
                prv, cur, nxt, prv, cur, nxt,
                pl.BlockSpec(wbias.shape, lambda i, pv, nx, fi, la: (0, 0, 0)),
                pl.BlockSpec((1, QA_W), lambda i, pv, nx, fi, la: (0, 0)),
            ],
            out_specs=pl.BlockSpec((tq, QA_W), lambda i, pv, nx, fi, la: (i, 0)),
            scratch_shapes=[pltpu.VMEM((tq + 2 * WBLK, KVA_W), BF16),
                            pltpu.VMEM((tq + 2 * WBLK, KVA_W + LANES), BF16)],
        ),
        out_shape=jax.ShapeDtypeStruct((t, QA_W), BF16),
        compiler_params=_params(),
        name="window_attn",
    )(prev_blk, next_blk, first, last, sink, qa, ka, ka, ka, va, va, va, wbias, gnorm)


def _nbr_kernel(prev_ref, next_ref, row0_ref, rows_ref, q_ref, kp_ref, kc_ref, kn_ref,
                vp_ref, vc_ref, vn_ref, bias_ref, g_ref, o_ref, kwin, vwin):
    del prev_ref, next_ref
    j = pl.program_id(0)
    chunk = NA_CHUNK_ROWS * GRID_W
    ones = jnp.ones((3 * chunk, LANES), BF16)
    for c in range(B_W // LANES):
        vwin[:, (2 * c + 1) * LANES:(2 * c + 2) * LANES] = ones
    for w, (kr, vr) in enumerate(((kp_ref, vp_ref), (kc_ref, vc_ref), (kn_ref, vn_ref))):
        kwin[w * chunk:(w + 1) * chunk, :] = kr[...]
        for c in range(B_W // LANES):
            vwin[w * chunk:(w + 1) * chunk, 2 * c * LANES:(2 * c + 1) * LANES] = vr[:, c * LANES:(c + 1) * LANES]
    row0 = row0_ref[j]
    rows = rows_ref[j]
    g = g_ref[...]
    nkeys = NA_ROWS * GRID_W

    def body(rr, carry):
        r = row0 + rr
        rstart = jnp.clip(r - NA_ROWS // 2, 0, rows - NA_ROWS)
        bidx = (NA_ROWS - 1) - (r - rstart)
        loc = pl.multiple_of((rstart - row0 + NA_CHUNK_ROWS) * GRID_W, GRID_W)
        qoff = pl.multiple_of(rr * GRID_W, GRID_W)
        q = q_ref[pl.ds(qoff, GRID_W), :]
        outs = []
        for c in range(B_W // LANES):
            kc = kwin[pl.ds(loc, nkeys), c * LANES:(c + 1) * LANES]
            vc = vwin[pl.ds(loc, nkeys), 2 * c * LANES:(2 * c + 2) * LANES]
            s = _dot_nt(_split_heads(q[:, c * LANES:(c + 1) * LANES]), kc)
            s = s + jnp.concatenate([bias_ref[c, bidx + 2 * jj] for jj in range(NA_ROWS // 2)], axis=-1)
            m = jnp.max(s, axis=-1, keepdims=True)
            p = jnp.exp2(s - m).astype(BF16)
            of = jnp.dot(p, vc, preferred_element_type=F32)
            outs.append(_merge_heads(of[:, :LANES] / of[:, LANES:]))
        o = jnp.concatenate(outs, axis=-1)
        o_ref[pl.ds(qoff, GRID_W), :] = _rmsnorm(o, g).astype(BF16)
        return carry

    lax.fori_loop(0, NA_CHUNK_ROWS, body, 0, unroll=True)


def _nbr_attn(qb, kb, vb, bias_pairs, gnorm, tables):
    t = qb.shape[0]
    prev_c, next_c, row0, rows = tables
    chunk = NA_CHUNK_ROWS * GRID_W
    cur = pl.BlockSpec((chunk, B_W), lambda i, pv, nx, r0, rs: (i, 0))
    prv = pl.BlockSpec((chunk, B_W), lambda i, pv, nx, r0, rs: (pv[i], 0))
    nxt = pl.BlockSpec((chunk, B_W), lambda i, pv, nx, r0, rs: (nx[i], 0))
    return pl.pallas_call(
        _nbr_kernel,
        grid_spec=pltpu.PrefetchScalarGridSpec(
            num_scalar_prefetch=4,
            grid=(t // chunk,),
            in_specs=[
                cur, prv, cur, nxt, prv, cur, nxt,
                pl.BlockSpec(bias_pairs.shape, lambda i, pv, nx, r0, rs: (0, 0, 0, 0)),
                pl.BlockSpec((1, B_W), lambda i, pv, nx, r0, rs: (0, 0)),
            ],
            out_specs=cur,
            scratch_shapes=[pltpu.VMEM((3 * chunk, B_W), BF16), pltpu.VMEM((3 * chunk, 2 * B_W), BF16)],
        ),
        out_shape=jax.ShapeDtypeStruct((t, B_W), BF16),
        compiler_params=_params(vmem=V7X_VMEM_LIMIT_BYTES),
        name="nbr_attn",
    )(prev_c, next_c, row0, rows, qb, kb, kb, kb, vb, vb, vb, bias_pairs, gnorm)


def _route_tile(xn, rt_ref, w_ref, lp_ref, cnt_ref):
    tile = xn.shape[0]
    rt = rt_ref[...]
    rt_hi = rt.astype(BF16)
    rt_lo = (rt - rt_hi.astype(F32)).astype(BF16)
    xn_hi = xn.astype(BF16)
    xn_lo = (xn - xn_hi.astype(F32)).astype(BF16)
    part = _dot_nt(jnp.concatenate([rt_hi, rt_lo], axis=0), xn_hi)
    logits = part[:N_EXPERTS] + part[N_EXPERTS:] + _dot_nt(rt_hi, xn_lo)
    e = lax.broadcasted_iota(jnp.int32, logits.shape, 0)
    m1 = jnp.max(logits, axis=0, keepdims=True)
    i1 = jnp.min(jnp.where(logits == m1, e, N_EXPERTS), axis=0, keepdims=True)
    rest = jnp.where(e == i1, -jnp.inf, logits)
    m2 = jnp.max(rest, axis=0, keepdims=True)
    i2 = jnp.min(jnp.where(rest == m2, e, N_EXPERTS), axis=0, keepdims=True)
    t2 = jnp.exp(m2 - m1)
    w_ref[...] = jnp.concatenate([1.0 / (1.0 + t2), t2 / (1.0 + t2)], axis=0)
    oh1 = jnp.where(e == i1, 1.0, 0.0)
    oh2 = jnp.where(e == i2, 1.0, 0.0)
    both = oh1 + oh2
    earlier = jnp.where(lax.broadcasted_iota(jnp.int32, (tile, tile), 0)
                        < lax.broadcasted_iota(jnp.int32, (tile, tile), 1), 1.0, 0.0).astype(BF16)
    before = jnp.dot(both.astype(BF16), earlier, preferred_element_type=F32)
    count = jnp.sum(both, axis=1, keepdims=True).astype(jnp.int32)
    rows = ((count + (SLAB_ALIGN - 1)) // SLAB_ALIGN) * SLAB_ALIGN
    erow = lax.broadcasted_iota(jnp.int32, rows.shape, 0)
    start = jnp.zeros_like(rows)
    for k in range(N_EXPERTS - 1):
        start = start + jnp.where(erow > k, rows[k:k + 1, :], 0)
    where = start.astype(F32) + before
    lp_ref[...] = jnp.concatenate([jnp.sum(oh1 * where, axis=0, keepdims=True),
                                   jnp.sum(oh2 * where, axis=0, keepdims=True)], axis=0).astype(jnp.int32)
    cnt_ref[0] = jnp.broadcast_to(rows, cnt_ref.shape[1:])


def _outproj_kernel(x_ref, oa_ref, ob_ref, wa_ref, wb_ref, g_ref, b_ref, *rest, alpha, route):
    mix = jnp.dot(oa_ref[...], wa_ref[...], preferred_element_type=F32)
    mix = mix + jnp.dot(ob_ref[...], wb_ref[...], preferred_element_type=F32)
    xn = _layernorm(alpha * x_ref[...] + mix, g_ref[...], b_ref[...])
    if route:
        rt_ref, o_ref, w_ref, lp_ref, cnt_ref = rest
        _route_tile(xn, rt_ref, w_ref, lp_ref, cnt_ref)
    else:
        o_ref, = rest
    o_ref[...] = xn


def _outproj(x, oa, ob, w_o, layer, g, b, alpha, router_t=None):
    t = x.shape[0]
    tm = ROW_TILE
    assert tm == MOE_TILE
    row = lambda i: (i, 0)
    const = lambda i: (0, 0)
    in_specs = [
        pl.BlockSpec((tm, D_MODEL), row),
        pl.BlockSpec((tm, QA_W), row),
        pl.BlockSpec((tm, B_W), row),
        pl.BlockSpec((None, QA_W, D_MODEL), lambda i: (layer, 0, 0)),
        pl.BlockSpec((None, B_W, D_MODEL), lambda i: (layer, QA_W // B_W, 0)),
        pl.BlockSpec((1, D_MODEL), const),
        pl.BlockSpec((1, D_MODEL), const),
    ]
    out_specs = [pl.BlockSpec((tm, D_MODEL), row)]
    out_shape = [jax.ShapeDtypeStruct((t, D_MODEL), F32)]
    args = [x, oa, ob, w_o, w_o, g, b]
    if router_t is not None:
        in_specs.append(pl.BlockSpec((N_EXPERTS, D_MODEL), const))
        args.append(router_t)
        out_specs += [pl.BlockSpec((TOP_K, tm), lambda i: (0, i)), pl.BlockSpec((TOP_K, tm), lambda i: (0, i)),
                      pl.BlockSpec((1, N_EXPERTS, LANES), lambda i: (i, 0, 0))]
        out_shape += [jax.ShapeDtypeStruct((TOP_K, t), F32), jax.ShapeDtypeStruct((TOP_K, t), jnp.int32),
                      jax.ShapeDtypeStruct((t // tm, N_EXPERTS, LANES), jnp.int32)]
    return pl.pallas_call(
        functools.partial(_outproj_kernel, alpha=alpha, route=router_t is not None),
        grid=(t // tm,),
        in_specs=in_specs,
        out_specs=out_specs,
        out_shape=out_shape,
        compiler_params=_params(vmem=V7X_VMEM_LIMIT_BYTES),
        name="out_proj_ln1",
    )(*args)


def _swiglu_tile(xb, wg_ref, wu_ref, wd_ref, h_ref):
    for c in range(D_FF // MXU_N):
        sl = slice(c * MXU_N, (c + 1) * MXU_N)
        gate = jnp.dot(xb, wg_ref[0, :, sl], preferred_element_type=F32)
        up = jnp.dot(xb, wu_ref[0, :, sl], preferred_element_type=F32)
        h_ref[:, sl] = (gate / (1.0 + jnp.exp(-gate)) * up).astype(BF16)
    return jnp.dot(h_ref[...], wd_ref[0], preferred_element_type=F32)


def _ffn_kernel(x_ref, wg_ref, wu_ref, wd_ref, g_ref, b_ref, o_ref, h_ref, *, alpha):
    x = x_ref[...]
    f = _swiglu_tile(x.astype(BF16), wg_ref, wu_ref, wd_ref, h_ref)
    o_ref[...] = _layernorm(alpha * x + f, g_ref[...], b_ref[...])


def _ffn_dense(x, wg, wu, wd, layer, g, b, alpha):
    t = x.shape[0]
    tm = ROW_TILE
    row = lambda i: (i, 0)
    const = lambda i: (0, 0)
    const3 = lambda i: (layer, 0, 0)
    return pl.pallas_call(
        functools.partial(_ffn_kernel, alpha=alpha),
        grid=(t // tm,),
        in_specs=[
            pl.BlockSpec((tm, D_MODEL), row),
            pl.BlockSpec((1, D_MODEL, D_FF), const3),
            pl.BlockSpec((1, D_MODEL, D_FF), const3),
            pl.BlockSpec((1, D_FF, D_MODEL), const3),
            pl.BlockSpec((1, D_MODEL), const),
            pl.BlockSpec((1, D_MODEL), const),
        ],
        out_specs=pl.BlockSpec((tm, D_MODEL), row),
        out_shape=jax.ShapeDtypeStruct((t, D_MODEL), F32),
        scratch_shapes=[pltpu.VMEM((tm, D_FF), BF16)],
        compiler_params=_params(vmem=V7X_VMEM_LIMIT_BYTES),
        name="ffn_dense_ln2",
    )(x, wg, wu, wd, g, b)


def _slab_copies(n_rows, src, src_row, dst, dst_row, sem, action, max_rows):
    shift = SLAB_ALIGN.bit_length() - 1
    for b in reversed(range((max_rows // SLAB_ALIGN).bit_length())):
        size = SLAB_ALIGN << b
        if size > max_rows:
            continue
        done = (n_rows >> (shift + b + 1)) << (shift + b + 1)

        @pl.when(((n_rows >> (shift + b)) & 1) == 1)
        def _(size=size, done=done):
            s = pl.multiple_of(src_row + done, SLAB_ALIGN)
            d = pl.multiple_of(dst_row + done, SLAB_ALIGN)
            action(pltpu.make_async_copy(src.at[pl.ds(s, size)], dst.at[pl.ds(d, size)], sem))


def _start(copy):
    copy.start()


def _wait(copy):
    copy.wait()


def _dispatch_kernel(n_ref, off_ref, loff_ref, padoff_ref, padlen_ref, nvalid_ref, x_ref, lp_ref, xs_hbm,
                     slab, zeros, sems):
    i = pl.program_id(0)
    slot = i % 2

    @pl.when(i == 0)
    def _():
        zeros[...] = jnp.zeros_like(zeros)
        tile_rows = zeros.shape[0]
        sem = sems.at[2]
        for action in (_start, _wait):
            for e in range(N_EXPERTS):
                _slab_copies(padlen_ref[e], zeros, 0, xs_hbm, padoff_ref[e], sem, action, tile_rows)

            def unused_tile(j, carry, action=action):
                row = pl.multiple_of(j * tile_rows, tile_rows)
                action(pltpu.make_async_copy(zeros, xs_hbm.at[pl.ds(row, tile_rows)], sem))
                return carry
            lax.fori_loop(nvalid_ref[0], xs_hbm.shape[0] // tile_rows, unused_tile, 0)

    def copies(tile, sl, action):
        for e in range(N_EXPERTS):
            k = tile * N_EXPERTS + e
            _slab_copies(n_ref[k], slab.at[sl], loff_ref[k], xs_hbm, off_ref[k], sems.at[sl], action, MOE_TILE)

    lp = lp_ref[...]
    q = lax.broadcasted_iota(jnp.int32, (slab.shape[1], lp.shape[1]), 0)
    sel = jnp.where(q == lp[0:1, :], 1.0, jnp.where(q == lp[1:2, :], 1.0, 0.0))
    slab[slot] = jnp.dot(sel.astype(BF16), x_ref[...].astype(BF16), preferred_element_type=F32).astype(BF16)
    copies(i, slot, _start)

    @pl.when(i > 0)
    def _():
        copies(i - 1, 1 - slot, _wait)

    @pl.when(i == pl.num_programs(0) - 1)
    def _():
        copies(i, slot, _wait)


def _moe_dispatch(x, lp, route, n_rows):
    t = x.shape[0]
    tt = MOE_TILE
    imap2 = lambda i, *_: (i, 0)
    return pl.pallas_call(
        _dispatch_kernel,
        grid_spec=pltpu.PrefetchScalarGridSpec(
            num_scalar_prefetch=6,
            grid=(t // tt,),
            in_specs=[
                pl.BlockSpec((tt, D_MODEL), imap2),
                pl.BlockSpec((TOP_K, tt), lambda i, *_: (0, i)),
            ],
            out_specs=pl.BlockSpec(memory_space=pl.ANY),
            scratch_shapes=[
                pltpu.VMEM((2, SLAB_ROWS, D_MODEL), BF16),
                pltpu.VMEM((ROW_TILE, D_MODEL), BF16),
                pltpu.SemaphoreType.DMA((3,)),
            ],
        ),
        out_shape=jax.ShapeDtypeStruct((n_rows, D_MODEL), BF16),
        compiler_params=_params(vmem=V7X_VMEM_LIMIT_BYTES),
        name="moe_dispatch",
    )(route["n"], route["off"], route["loff"], route["pad_off"], route["pad_len"], route["nvalid"], x,
      lp)


def _moe_kernel(tile_e_ref, nvalid_ref, x_ref, wg_ref, wu_ref, wd_ref, y_ref, h_ref):
    del tile_e_ref
    j = pl.program_id(0)
    nv = nvalid_ref[0]

    @pl.when(j < nv)
    def _():
        y_ref[...] = _swiglu_tile(x_ref[...], wg_ref, wu_ref, wd_ref, h_ref).astype(BF16)

    @pl.when(j >= nv)
    def _():
        y_ref[...] = jnp.zeros_like(y_ref)


def _moe_experts(xs, tile_e, nvalid, wg, wu, wd, layer):
    tm = ROW_TILE
    n_tiles = xs.shape[0] // tm
    wspec = lambda shape: pl.BlockSpec((None, 1) + shape, lambda j, te, nv: (layer, te[j], 0, 0))
    return pl.pallas_call(
        _moe_kernel,
        grid_spec=pltpu.PrefetchScalarGridSpec(
            num_scalar_prefetch=2,
            grid=(n_tiles,),
            in_specs=[
                pl.BlockSpec((tm, D_MODEL), lambda j, te, nv: (jnp.minimum(j, nv[0] - 1), 0)),
                wspec((D_MODEL, D_FF)), wspec((D_MODEL, D_FF)), wspec((D_FF, D_MODEL)),
            ],
            out_specs=pl.BlockSpec((tm, D_MODEL), lambda j, te, nv: (j, 0)),
            scratch_shapes=[pltpu.VMEM((tm, D_FF), BF16)],
        ),
        out_shape=jax.ShapeDtypeStruct(xs.shape, BF16),
        compiler_params=_params(vmem=V7X_VMEM_LIMIT_BYTES),
        name="moe_experts",
    )(tile_e, nvalid, xs, wg, wu, wd)


def _combine_kernel(n_ref, off_ref, loff_ref, x_ref, lp_ref, w_ref, ys_hbm, g_ref, b_ref, *rest, alpha, split):
    out_refs, (yslab, sems) = rest[:-2], rest[-2:]
    i = pl.program_id(0)
    slot = i % 2

    def copies(tile, sl, action):
        for e in range(N_EXPERTS):
            k = tile * N_EXPERTS + e
            _slab_copies(n_ref[k], ys_hbm, off_ref[k], yslab.at[sl], loff_ref[k], sems.at[sl], action, MOE_TILE)

    @pl.when(i == 0)
    def _():
        yslab[...] = jnp.zeros_like(yslab)
        copies(0, 0, _start)

    @pl.when(i + 1 < pl.num_programs(0))
    def _():
        copies(i + 1, 1 - slot, _start)

    copies(i, slot, _wait)
    lp = lp_ref[...]
    w = w_ref[...]
    q = lax.broadcasted_iota(jnp.int32, (yslab.shape[1], lp.shape[1]), 0)
    gate_t = jnp.where(q == lp[0:1, :], w[0:1, :], 0.0) + jnp.where(q == lp[1:2, :], w[1:2, :], 0.0)
    f = lax.dot_general(gate_t.astype(BF16), yslab[slot], (((0,), (0,)), ((), ())),
                        preferred_element_type=F32)
    o = _layernorm(alpha * x_ref[...] + f, g_ref[...], b_ref[...])
    if split is None:
        out_refs[0][...] = o
    else:
        @pl.when(i < split)
        def _():
            out_refs[0][...] = o

        @pl.when(i >= split)
        def _():
            out_refs[1][...] = o


def _moe_combine(x, gate_w, lp, route, y_sorted, g, b, alpha, split_rows=None):
    t = x.shape[0]
    tt = MOE_TILE
    n = t // tt
    imap2 = lambda i, *_: (i, 0)
    const = lambda i, *_: (0, 0)
    if split_rows is None:
        split = None
        out_specs = pl.BlockSpec((tt, D_MODEL), imap2)
        out_shape = jax.ShapeDtypeStruct((t, D_MODEL), F32)
    else:
        assert split_rows % tt == 0 and 0 < split_rows < t
        split = split_rows // tt
        out_specs = [pl.BlockSpec((tt, D_MODEL), lambda i, *_: (jnp.minimum(i, split - 1), 0)),
                     pl.BlockSpec((tt, D_MODEL), lambda i, *_: (jnp.maximum(i - split, 0), 0))]
        out_shape = [jax.ShapeDtypeStruct((split_rows, D_MODEL), F32),
                     jax.ShapeDtypeStruct((t - split_rows, D_MODEL), F32)]
    return pl.pallas_call(
        functools.partial(_combine_kernel, alpha=alpha, split=split),
        grid_spec=pltpu.PrefetchScalarGridSpec(
            num_scalar_prefetch=3,
            grid=(n,),
            in_specs=[
                pl.BlockSpec((tt, D_MODEL), imap2),
                pl.BlockSpec((TOP_K, tt), lambda i, *_: (0, i)),
                pl.BlockSpec((TOP_K, tt), lambda i, *_: (0, i)),
                pl.BlockSpec(memory_space=pl.ANY),
                pl.BlockSpec((1, D_MODEL), const),
                pl.BlockSpec((1, D_MODEL), const),
            ],
            out_specs=out_specs,
            scratch_shapes=[pltpu.VMEM((2, SLAB_ROWS, D_MODEL), BF16), pltpu.SemaphoreType.DMA((2,))],
        ),
        out_shape=out_shape,
        compiler_params=_params(vmem=V7X_VMEM_LIMIT_BYTES),
        name="moe_combine_ln2",
    )(route["n"], route["off"], route["loff"], x, lp, gate_w, y_sorted, g, b)


def _route(n, t):
    i32 = jnp.int32
    loff = jnp.cumsum(n, axis=1) - n
    seg_rows = jnp.sum(n, axis=0)
    seg_len = ((seg_rows + ROW_TILE - 1) // ROW_TILE) * ROW_TILE
    ends = jnp.cumsum(seg_len)
    goff = ends - seg_len
    off = goff[None, :] + jnp.cumsum(n, axis=0) - n
    n_tiles = _sorted_rows(t) // ROW_TILE
    nvalid = ends[-1] // ROW_TILE
    starts = jnp.arange(n_tiles, dtype=i32) * ROW_TILE
    tile_e = jnp.minimum(jnp.sum((starts[:, None] >= ends[None, :]).astype(i32), axis=1), N_EXPERTS - 1)
    tile_e = jnp.where(jnp.arange(n_tiles) < nvalid, tile_e, tile_e[nvalid - 1])
    flat = lambda v: v.reshape(-1).astype(i32)
    return dict(n=flat(n), off=flat(off), loff=flat(loff), pad_off=flat(goff + seg_rows),
                pad_len=flat(seg_len - seg_rows), tile_e=flat(tile_e), nvalid=flat(nvalid))


def _sorted_rows(t):
    worst = t * TOP_K + (t // MOE_TILE) * N_EXPERTS * (SLAB_ALIGN - 1) + N_EXPERTS * (ROW_TILE - 1)
    return -(-worst // ROW_TILE) * ROW_TILE


def _moe_layer(x, gate_w, lp, slab_rows, wg, wu, wd, layer, g, b, alpha, split_rows=None):
    t = x.shape[0]
    route = _route(slab_rows[:, :, 0], t)
    xs = _moe_dispatch(x, lp, route, _sorted_rows(t))
    ys = _moe_experts(xs, route["tile_e"], route["nvalid"], wg, wu, wd, layer)
    return _moe_combine(x, gate_w, lp, route, ys, g, b, alpha, split_rows)


def _segment_tables(groups, unit):
    prev, nxt, within, per_seq = [], [], [], []
    base = 0
    for n_seq, seq_len in groups:
        nb = seq_len // unit
        for _ in range(n_seq):
            for n in range(nb):
                i = base + n
                prev.append(i - 1 if n > 0 else i)
                nxt.append(i + 1 if n < nb - 1 else i)
                within.append(n)
                per_seq.append(nb)
            base += nb
    as_i32 = lambda v: jnp.asarray(np.asarray(v, np.int32))
    return as_i32(prev), as_i32(nxt), as_i32(within), as_i32(per_seq)


def _window_tables(groups):
    prev, nxt, within, per_seq = _segment_tables(groups, ROW_TILE)
    per_tile = ROW_TILE // WBLK
    tile = jnp.arange(prev.shape[0], dtype=jnp.int32)
    first = (within == 0).astype(jnp.int32)
    last = (within == per_seq - 1).astype(jnp.int32)
    prev_blk = jnp.where(first == 1, tile * per_tile, tile * per_tile - 1)
    next_blk = jnp.where(last == 1, tile * per_tile + per_tile - 1, tile * per_tile + per_tile)
    i = np.arange(WBLK)[:, None]
    jj = np.arange(3 * WBLK)[None, :]
    band = np.abs(jj - WBLK - i) <= WINDOW
    masks = []
    for no_prev, no_next in ((True, False), (False, False), (False, True)):
        inr = np.ones_like(band)
        if no_prev:
            inr = inr & (jj >= WBLK)
        if no_next:
            inr = inr & (jj < 2 * WBLK)
        masks.append(np.where(band & inr, 0.0, NEG))
    return prev_blk, next_blk, first, last, jnp.asarray(np.stack(masks).astype(np.float32))


def _nbr_tables(groups):
    chunk = NA_CHUNK_ROWS * GRID_W
    prev, nxt, within, per_seq = _segment_tables(groups, chunk)
    return prev, nxt, within * NA_CHUNK_ROWS, per_seq * NA_CHUNK_ROWS


def _nbr_bias_pairs(na_rpb):
    c = np.arange(GRID_W)[:, None]
    kc = np.arange(GRID_W)[None, :]
    cstart = np.clip(c - NA_COLS // 2, 0, GRID_W - NA_COLS)
    allowed = (kc >= cstart) & (kc < cstart + NA_COLS)
    dc = np.clip(kc - c + (NA_COLS - 1), 0, 2 * NA_COLS - 2)
    pick = jnp.asarray((dc[:, :, None] == np.arange(2 * NA_COLS - 1)).astype(np.float32))
    looked_up = jnp.einsum('lhrd,ckd->lhrck', na_rpb.astype(F32), pick, precision=lax.Precision.HIGHEST)
    full = jnp.where(jnp.asarray(allowed)[None, None, None], looked_up * LOG2E, NEG)
    pairs = jnp.concatenate([full[:, :, :-1], full[:, :, 1:]], axis=-1)
    nl, nh, nd = pairs.shape[:3]
    pairs = pairs.reshape(nl, nh // 2, 2, nd, GRID_W, 2 * GRID_W).transpose(0, 1, 3, 2, 4, 5)
    return pairs.reshape(nl, nh // 2, nd, 2 * GRID_W, 2 * GRID_W)


def _rope_tables(max_len):
    half = HEAD_DIM // 2
    inv = 1.0 / (ROPE_THETA ** (jnp.arange(half, dtype=F32) / half))
    ang = jnp.arange(max_len, dtype=F32)[:, None] * inv[None, :]
    cos, sin = jnp.cos(ang), jnp.sin(ang)
    reps = LANES // HEAD_DIM
    return jnp.tile(jnp.concatenate([cos, cos], axis=-1), (1, reps)), \
        jnp.tile(jnp.concatenate([-sin, sin], axis=-1), (1, reps))


def _trunk(xa, xb, groups, emb_ln_g, emb_ln_b, w_in, attn_sink, na_rpb, gnorm_a, gnorm_b, w_o, ln1_g, ln1_b,
           ffn_gate, ffn_up, ffn_down, router, exp_gate, exp_up, exp_down, ln2_g, ln2_b):
    depth = w_in.shape[0]
    alpha = (2.0 * depth) ** 0.25
    for n_seq, seq_len in groups:
        assert seq_len % (NA_CHUNK_ROWS * GRID_W) == 0 and seq_len // WBLK >= 2
        assert (n_seq * seq_len) % ROW_TILE == 0 and seq_len % ROW_TILE == 0
        assert (n_seq * seq_len) % MOE_TILE == 0
    assert depth % 2 == 0, "the last layer must be a routed one: its combine kernel splits the outputs"
    row2 = lambda v: v.reshape(1, -1)

    win_tables = _window_tables(groups)
    nbr_tables = _nbr_tables(groups)
    cos_t, sin_t = _rope_tables(max(s for _, s in groups))
    _, _, pos_within, _ = _segment_tables(groups, ROW_TILE)
    bias_pairs = _nbr_bias_pairs(na_rpb)
    def pair_heads(w, axis):
        shape = w.shape[:axis] + (HKV_A, GQA, HEAD_DIM) + w.shape[axis + 1:]
        return jnp.swapaxes(w.reshape(shape), axis, axis + 1).reshape(w.shape)

    w_in_b = jnp.concatenate([pair_heads(w_in[:, :, :QA_W], 2), w_in[:, :, QA_W:]], axis=2).astype(BF16)
    w_o_b = jnp.concatenate([pair_heads(w_o[:, :QA_W], 1), w_o[:, QA_W:]], axis=1).astype(BF16)
    gnorm_a = pair_heads(gnorm_a, 1)
    ffn_b = [w.astype(BF16) for w in (ffn_gate, ffn_up, ffn_down)]
    exp_b = [w.astype(BF16) for w in (exp_gate, exp_up, exp_down)]
    router_t = jnp.swapaxes(router, 1, 2)

    for l in range(depth):
        if l == 0:
            qa, ka, va, qb, kb, vb, x = _inproj(xa, w_in_b, l, cos_t, sin_t, pos_within,
                                                embed=(xb, row2(emb_ln_g), row2(emb_ln_b)))
        else:
            qa, ka, va, qb, kb, vb = _inproj(x, w_in_b, l, cos_t, sin_t, pos_within)
        oa = _window_attn(qa, ka, va, attn_sink[l], row2(gnorm_a[l]), win_tables)
        ob = _nbr_attn(qb, kb, vb, bias_pairs[l], row2(gnorm_b[l]), nbr_tables)
        i = l // 2
        if l % 2 == 0:
            x, = _outproj(x, oa, ob, w_o_b, l, row2(ln1_g[l]), row2(ln1_b[l]), alpha)
            x = _ffn_dense(x, *ffn_b, i, row2(ln2_g[l]), row2(ln2_b[l]), alpha)
        else:
            x, gate_w, lp, slab_rows = _outproj(x, oa, ob, w_o_b, l, row2(ln1_g[l]), row2(ln1_b[l]), alpha,
                                                router_t=router_t[i])
            x = _moe_layer(x, gate_w, lp, slab_rows, *exp_b, i, row2(ln2_g[l]), row2(ln2_b[l]), alpha,
                           split_rows=xa.shape[0] if l == depth - 1 else None)
    return x


def kernel(x_prompt, x_sample, emb_ln_g, emb_ln_b, w_in, attn_sink, na_rpb, gnorm_a, gnorm_b, w_o, ln1_g, ln1_b,
           ffn_gate, ffn_up, ffn_down, router, exp_gate, exp_up, exp_down, ln2_g, ln2_b):
    groups = (x_prompt.shape[:2], x_sample.shape[:2])
    ta = x_prompt.shape[0] * x_prompt.shape[1]
    ya, yb = _trunk(x_prompt.reshape(ta, D_MODEL), x_sample.reshape(-1, D_MODEL), groups,
                    emb_ln_g, emb_ln_b, w_in, attn_sink, na_rpb, gnorm_a, gnorm_b, w_o, ln1_g, ln1_b,
                    ffn_gate, ffn_up, ffn_down, router, exp_gate, exp_up, exp_down, ln2_g, ln2_b)
    return ya.reshape(x_prompt.shape), yb.reshape(x_sample.shape)
```

```python
import functools

import numpy as np
import jax
import jax.numpy as jnp
from jax import lax
from jax.experimental import pallas as pl
from jax.experimental.pallas import tpu as pltpu

D_MODEL = 1024
HEAD_DIM = 64
H_A = 8
HKV_A = 2
GQA = H_A // HKV_A
H_B = 8
WINDOW = 128
WBLK = 128
ROPE_THETA = 10000.0
GRID_W = 64
NA_ROWS = 8
NA_COLS = 16
D_FF = 2816
N_EXPERTS = 8
TOP_K = 2
LN_EPS = 1e-5
NEG = -1e30
LOG2E = 1.4426950408889634
QA_W = H_A * HEAD_DIM
KVA_W = HKV_A * HEAD_DIM
B_W = H_B * HEAD_DIM
IN_W = QA_W + 2 * KVA_W + 3 * B_W
ROPE_W = QA_W + KVA_W

V7X_VMEM_LIMIT_BYTES = 56 * 1024 * 1024
LANES = 128
MXU_N = 256
ROW_TILE = 512
NA_CHUNK_ROWS = 8
NA_WINDOW_CHUNKS = 3
MOE_TILE = 512
SLAB_ALIGN = 16
SLAB_ROWS = 1152

F32 = jnp.float32
BF16 = jnp.bfloat16


def _layernorm(y, g, b):
    mu = jnp.mean(y, axis=-1, keepdims=True)
    yc = y - mu
    var = jnp.mean(yc * yc, axis=-1, keepdims=True)
    return yc * lax.rsqrt(var + LN_EPS) * g + b


def _rmsnorm(o, g):
    ms = jnp.mean(o * o, axis=-1, keepdims=True)
    return o * lax.rsqrt(ms + LN_EPS) * g


def _dot_nt(a, b):
    return lax.dot_general(a, b, (((1,), (1,)), ((), ())), preferred_element_type=F32)


def _params(n_axes=1, vmem=None):
    return pltpu.CompilerParams(dimension_semantics=("arbitrary",) * n_axes, vmem_limit_bytes=vmem)


def _inproj_kernel(pos_ref, *refs, embed_tiles):
    del pos_ref
    if embed_tiles is None:
        x_ref, w_ref, cos_ref, sin_ref, qa_ref, ka_ref, va_ref, qb_ref, kb_ref, vb_ref = refs
        x = x_ref[...]
    else:
        (xa_ref, xb_ref, eg_ref, eb_ref, w_ref, cos_ref, sin_ref,
         qa_ref, ka_ref, va_ref, qb_ref, kb_ref, vb_ref, xln_ref) = refs
        raw = jnp.where(pl.program_id(0) < embed_tiles, xa_ref[...], xb_ref[...])
        x = _layernorm(raw, eg_ref[...], eb_ref[...])
        xln_ref[...] = x
    xb = x.astype(BF16)
    cos = cos_ref[...]
    sin = sin_ref[...]
    lane = lax.broadcasted_iota(jnp.int32, cos.shape, 1)
    first_half = (lane % HEAD_DIM) < (HEAD_DIM // 2)
    scale = HEAD_DIM ** -0.5 * LOG2E

    def rope(h):
        partner = jnp.where(first_half, pltpu.roll(h, LANES - HEAD_DIM // 2, 1),
                            pltpu.roll(h, HEAD_DIM // 2, 1))
        return h * cos + partner * sin

    for c in range(QA_W // MXU_N):
        h = jnp.dot(xb, w_ref[:, c * MXU_N:(c + 1) * MXU_N], preferred_element_type=F32)
        for half in range(MXU_N // LANES):
            lo = c * MXU_N + half * LANES
            qa_ref[:, lo:lo + LANES] = (rope(h[:, half * LANES:(half + 1) * LANES]) * scale).astype(BF16)
    h = jnp.dot(xb, w_ref[:, QA_W:QA_W + 2 * KVA_W], preferred_element_type=F32)
    ka_ref[...] = rope(h[:, :KVA_W]).astype(BF16)
    va_ref[...] = h[:, KVA_W:].astype(BF16)
    off = QA_W + 2 * KVA_W
    for j, (ref, s) in enumerate(((qb_ref, scale), (kb_ref, 1.0), (vb_ref, 1.0))):
        for c in range(B_W // MXU_N):
            lo = off + j * B_W + c * MXU_N
            h = jnp.dot(xb, w_ref[:, lo:lo + MXU_N], preferred_element_type=F32)
            ref[:, c * MXU_N:(c + 1) * MXU_N] = (h * s).astype(BF16)


def _inproj(x, w_bf16, layer, cos_t, sin_t, pos_blk, embed=None):
    tm = ROW_TILE
    row = lambda i, p: (i, 0)
    const = lambda i, p: (0, 0)
    if embed is None:
        t = x.shape[0]
        embed_tiles = None
        x_specs = [pl.BlockSpec((tm, D_MODEL), row)]
        x_args = [x]
    else:
        xb, eg, eb = embed
        t = x.shape[0] + xb.shape[0]
        embed_tiles = x.shape[0] // tm
        x_specs = [pl.BlockSpec((tm, D_MODEL), lambda i, p: (jnp.minimum(i, embed_tiles - 1), 0)),
                   pl.BlockSpec((tm, D_MODEL), lambda i, p: (jnp.maximum(i - embed_tiles, 0), 0)),
                   pl.BlockSpec((1, D_MODEL), const), pl.BlockSpec((1, D_MODEL), const)]
        x_args = [x, xb, eg, eb]
    out_shapes = [jax.ShapeDtypeStruct((t, w), BF16) for w in (QA_W, KVA_W, KVA_W, B_W, B_W, B_W)]
    if embed is not None:
        out_shapes.append(jax.ShapeDtypeStruct((t, D_MODEL), F32))
    return pl.pallas_call(
        functools.partial(_inproj_kernel, embed_tiles=embed_tiles),
        grid_spec=pltpu.PrefetchScalarGridSpec(
            num_scalar_prefetch=1,
            grid=(t // tm,),
            in_specs=x_specs + [
                pl.BlockSpec((None, D_MODEL, IN_W), lambda i, p: (layer, 0, 0)),
                pl.BlockSpec((tm, LANES), lambda i, p: (p[i], 0)),
                pl.BlockSpec((tm, LANES), lambda i, p: (p[i], 0)),
            ],
            out_specs=[pl.BlockSpec((tm, s.shape[1]), row) for s in out_shapes],
        ),
        out_shape=out_shapes,
        compiler_params=_params(vmem=V7X_VMEM_LIMIT_BYTES),
        name="in_proj_rope",
    )(pos_blk, *x_args, w_bf16, cos_t, sin_t)


def _split_heads(qc):
    lo = lax.broadcasted_iota(jnp.int32, qc.shape, 1) < HEAD_DIM
    zero = jnp.zeros_like(qc)
    return jnp.concatenate([jnp.where(lo, qc, zero), jnp.where(lo, zero, qc)], axis=0)


def _merge_heads(o2):
    m = o2.shape[0] // 2
    lo = lax.broadcasted_iota(jnp.int32, (m, LANES), 1) < HEAD_DIM
    return jnp.where(lo, o2[:m], o2[m:])


def _win_kernel(prev_ref, next_ref, first_ref, last_ref, sink_ref, q_ref, kp_ref, kc_ref, kn_ref,
                vp_ref, vc_ref, vn_ref, bias_ref, g_ref, o_ref, kwin, vwin):
    del prev_ref, next_ref
    i = pl.program_id(0)
    tq = q_ref.shape[0]
    nsub = tq // WBLK
    kwin[0:WBLK, :] = kp_ref[...]
    kwin[WBLK:WBLK + tq, :] = kc_ref[...]
    kwin[WBLK + tq:, :] = kn_ref[...]
    vwin[0:WBLK, 0:KVA_W] = vp_ref[...]
    vwin[WBLK:WBLK + tq, 0:KVA_W] = vc_ref[...]
    vwin[WBLK + tq:, 0:KVA_W] = vn_ref[...]
    vwin[:, KVA_W:] = jnp.ones((vwin.shape[0], LANES), BF16)
    first = first_ref[i]
    last = last_ref[i]
    g = g_ref[...]

    def body(j, carry):
        off = pl.multiple_of(j * WBLK, WBLK)
        bt = jnp.where((j == 0) & (first == 1), 0, jnp.where((j == nsub - 1) & (last == 1), 2, 1))
        bias = bias_ref[bt]
        k3 = kwin[pl.ds(off, 3 * WBLK), :]
        v3 = vwin[pl.ds(off, 3 * WBLK), :]
        q = q_ref[pl.ds(off, WBLK), :]
        outs = []
        for c in range(QA_W // LANES):
            qc = q[:, c * LANES:(c + 1) * LANES]
            lo = lax.broadcasted_iota(jnp.int32, qc.shape, 1) < HEAD_DIM
            halves = []
            for hh, qh in ((c, jnp.where(lo, qc, jnp.zeros_like(qc))), (GQA + c, jnp.where(lo, jnp.zeros_like(qc), qc))):
                s = _dot_nt(qh, k3) + bias
                sk = sink_ref[hh] * LOG2E
                m = jnp.maximum(jnp.max(s, axis=-1, keepdims=True), sk)
                p = jnp.exp2(s - m).astype(BF16)
                of = jnp.dot(p, v3, preferred_element_type=F32)
                halves.append(of[:, :LANES] / (of[:, LANES:] + jnp.exp2(sk - m)))
            outs.append(jnp.where(lo, halves[0], halves[1]))
        o = jnp.concatenate(outs, axis=-1)
        o_ref[pl.ds(off, WBLK), :] = _rmsnorm(o, g).astype(BF16)
        return carry

    lax.fori_loop(0, nsub, body, 0, unroll=True)


def _window_attn(qa, ka, va, sink, gnorm, tables):
    t = qa.shape[0]
    tq = ROW_TILE
    prev_blk, next_blk, first, last, wbias = tables
    cur = pl.BlockSpec((tq, KVA_W), lambda i, pv, nx, fi, la: (i, 0))
    prv = pl.BlockSpec((WBLK, KVA_W), lambda i, pv, nx, fi, la: (pv[i], 0))
    nxt = pl.BlockSpec((WBLK, KVA_W), lambda i, pv, nx, fi, la: (nx[i], 0))
    return pl.pallas_call(
        _win_kernel,
        grid_spec=pltpu.PrefetchScalarGridSpec(
            num_scalar_prefetch=4,
            grid=(t // tq,),
            in_specs=[
                pl.BlockSpec(memory_space=pltpu.SMEM),
                pl.BlockSpec((tq, QA_W), lambda i, pv, nx, fi, la: (i, 0)),
                prv, cur, nxt, prv, cur, nxt,
                pl.BlockSpec(wbias.shape, lambda i, pv, nx, fi, la: (0, 0, 0)),
                pl.BlockSpec((1, QA_W), lambda i, pv, nx, fi, la: (0, 0)),
            ],
            out_specs=pl.BlockSpec((tq, QA_W), lambda i, pv, nx, fi, la: (i, 0)),
            scratch_shapes=[pltpu.VMEM((tq + 2 * WBLK, KVA_W), BF16),
                            pltpu.VMEM((tq + 2 * WBLK, KVA_W + LANES), BF16)],
        ),
        out_shape=jax.ShapeDtypeStruct((t, QA_W), BF16),
        compiler_params=_params(),
        name="window_attn",
    )(prev_blk, next_blk, first, last, sink, qa, ka, ka, ka, va, va, va, wbias, gnorm)


def _nbr_kernel(wtok_ref, wrow_ref, row0_ref, rows_ref, q_ref, kwin, vwin, bias_ref, g_ref, o_ref):
    del wtok_ref
    j = pl.program_id(0)
    row0 = row0_ref[j]
    rows = rows_ref[j]
    wrow = wrow_ref[j]
    g = g_ref[...]
    nkeys = NA_ROWS * GRID_W
    ones = jnp.ones((nkeys, LANES), BF16)

    def body(rr, carry):
        r = row0 + rr
        rstart = jnp.clip(r - NA_ROWS // 2, 0, rows - NA_ROWS)
        bidx = (NA_ROWS - 1) - (r - rstart)
        loc = pl.multiple_of((rstart - wrow) * GRID_W, GRID_W)
        qoff = pl.multiple_of(rr * GRID_W, GRID_W)
        q = q_ref[pl.ds(qoff, GRID_W), :]
        outs = []
        for c in range(B_W // LANES):
            kc = kwin[pl.ds(loc, nkeys), c * LANES:(c + 1) * LANES]
            vc = jnp.concatenate([vwin[pl.ds(loc, nkeys), c * LANES:(c + 1) * LANES], ones], axis=-1)
            s = _dot_nt(_split_heads(q[:, c * LANES:(c + 1) * LANES]), kc)
            s = s + jnp.concatenate([bias_ref[c, bidx + 2 * jj] for jj in range(NA_ROWS // 2)], axis=-1)
            m = jnp.max(s, axis=-1, keepdims=True)
            p = jnp.exp2(s - m).astype(BF16)
            of = jnp.dot(p, vc, preferred_element_type=F32)
            outs.append(_merge_heads(of[:, :LANES] / of[:, LANES:]))
        o = jnp.concatenate(outs, axis=-1)
        o_ref[pl.ds(qoff, GRID_W), :] = _rmsnorm(o, g).astype(BF16)
        return carry

    lax.fori_loop(0, NA_CHUNK_ROWS, body, 0, unroll=True)


def _nbr_attn(qb, kb, vb, bias_pairs, gnorm, tables):
    t = qb.shape[0]
    win_tok, win_row, row0, rows = tables
    chunk = NA_CHUNK_ROWS * GRID_W
    cur = pl.BlockSpec((chunk, B_W), lambda i, wt, wr, r0, rs: (i, 0))
    win = pl.BlockSpec((pl.Element(NA_WINDOW_CHUNKS * chunk), pl.Element(B_W)),
                       lambda i, wt, wr, r0, rs: (wt[i] * chunk, 0))
    return pl.pallas_call(
        _nbr_kernel,
        grid_spec=pltpu.PrefetchScalarGridSpec(
            num_scalar_prefetch=4,
            grid=(t // chunk,),
            in_specs=[
                cur, win, win,
                pl.BlockSpec(bias_pairs.shape, lambda i, wt, wr, r0, rs: (0, 0, 0, 0)),
                pl.BlockSpec((1, B_W), lambda i, wt, wr, r0, rs: (0, 0)),
            ],
            out_specs=cur,
        ),
        out_shape=jax.ShapeDtypeStruct((t, B_W), BF16),
        compiler_params=_params(vmem=V7X_VMEM_LIMIT_BYTES),
        name="nbr_attn",
    )(win_tok, win_row, row0, rows, qb, kb, vb, bias_pairs, gnorm)


def _route_tile(xn, rt_ref, w_ref, lp_ref, cnt_ref):
    tile = xn.shape[0]
    rt = rt_ref[...]
    rt_hi = rt.astype(BF16)
    rt_lo = (rt - rt_hi.astype(F32)).astype(BF16)
    xn_hi = xn.astype(BF16)
    xn_lo = (xn - xn_hi.astype(F32)).astype(BF16)
    part = _dot_nt(jnp.concatenate([rt_hi, rt_lo], axis=0), xn_hi)
    logits = part[:N_EXPERTS] + part[N_EXPERTS:] + _dot_nt(rt_hi, xn_lo)
    e = lax.broadcasted_iota(jnp.int32, logits.shape, 0)
    m1 = jnp.max(logits, axis=0, keepdims=True)
    i1 = jnp.min(jnp.where(logits == m1, e, N_EXPERTS), axis=0, keepdims=True)
    rest = jnp.where(e == i1, -jnp.inf, logits)
    m2 = jnp.max(rest, axis=0, keepdims=True)
    i2 = jnp.min(jnp.where(rest == m2, e, N_EXPERTS), axis=0, keepdims=True)
    t2 = jnp.exp(m2 - m1)
    w_ref[...] = jnp.concatenate([1.0 / (1.0 + t2), t2 / (1.0 + t2)], axis=0)
    oh1 = jnp.where(e == i1, 1.0, 0.0)
    oh2 = jnp.where(e == i2, 1.0, 0.0)
    both = oh1 + oh2
    earlier = jnp.where(lax.broadcasted_iota(jnp.int32, (tile, tile), 0)
                        < lax.broadcasted_iota(jnp.int32, (tile, tile), 1), 1.0, 0.0).astype(BF16)
    before = jnp.dot(both.astype(BF16), earlier, preferred_element_type=F32)
    count = jnp.sum(both, axis=1, keepdims=True).astype(jnp.int32)
    rows = ((count + (SLAB_ALIGN - 1)) // SLAB_ALIGN) * SLAB_ALIGN
    erow = lax.broadcasted_iota(jnp.int32, rows.shape, 0)
    start = jnp.zeros_like(rows)
    for k in range(N_EXPERTS - 1):
        start = start + jnp.where(erow > k, rows[k:k + 1, :], 0)
    where = start.astype(F32) + before
    lp_ref[...] = jnp.concatenate([jnp.sum(oh1 * where, axis=0, keepdims=True),
                                   jnp.sum(oh2 * where, axis=0, keepdims=True)], axis=0).astype(jnp.int32)
    cnt_ref[0] = jnp.broadcast_to(rows, cnt_ref.shape[1:])


def _swiglu_tile(xb, wg_ref, wu_ref, wd_ref, h_ref):
    for c in range(D_FF // MXU_N):
        sl = slice(c * MXU_N, (c + 1) * MXU_N)
        gate = jnp.dot(xb, wg_ref[0, :, sl], preferred_element_type=F32)
        up = jnp.dot(xb, wu_ref[0, :, sl], preferred_element_type=F32)
        h_ref[:, sl] = (gate / (1.0 + jnp.exp(-gate)) * up).astype(BF16)
    return jnp.dot(h_ref[...], wd_ref[0], preferred_element_type=F32)


def _outproj_kernel(x_ref, oa_ref, ob_ref, wa_ref, wb_ref, g_ref, b_ref, *rest, alpha, mode):
    mix = jnp.dot(oa_ref[...], wa_ref[...], preferred_element_type=F32)
    mix = mix + jnp.dot(ob_ref[...], wb_ref[...], preferred_element_type=F32)
    xn = _layernorm(alpha * x_ref[...] + mix, g_ref[...], b_ref[...])
    if mode == "route":
        rt_ref, o_ref, w_ref, lp_ref, cnt_ref = rest
        _route_tile(xn, rt_ref, w_ref, lp_ref, cnt_ref)
        o_ref[...] = xn
    else:
        wg_ref, wu_ref, wd_ref, g2_ref, b2_ref, o_ref, h_ref = rest
        f = _swiglu_tile(xn.astype(BF16), wg_ref, wu_ref, wd_ref, h_ref)
        o_ref[...] = _layernorm(alpha * xn + f, g2_ref[...], b2_ref[...])


def _outproj(x, oa, ob, w_o, layer, g, b, alpha, router_t=None, ffn=None):
    t = x.shape[0]
    tm = ROW_TILE
    assert tm == MOE_TILE and (router_t is None) != (ffn is None)
    row = lambda i: (i, 0)
    const = lambda i: (0, 0)
    resident = dict(pipeline_mode=pl.Buffered(1))
    in_specs = [
        pl.BlockSpec((tm, D_MODEL), row),
        pl.BlockSpec((tm, QA_W), row),
        pl.BlockSpec((tm, B_W), row),
        pl.BlockSpec((None, QA_W, D_MODEL), lambda i: (layer, 0, 0), **resident),
        pl.BlockSpec((None, B_W, D_MODEL), lambda i: (layer, QA_W // B_W, 0), **resident),
        pl.BlockSpec((1, D_MODEL), const),
        pl.BlockSpec((1, D_MODEL), const),
    ]
    out_specs = [pl.BlockSpec((tm, D_MODEL), row)]
    out_shape = [jax.ShapeDtypeStruct((t, D_MODEL), F32)]
    args = [x, oa, ob, w_o, w_o, g, b]
    scratch = []
    if router_t is not None:
        in_specs.append(pl.BlockSpec((N_EXPERTS, D_MODEL), const))
        args.append(router_t)
        out_specs += [pl.BlockSpec((TOP_K, tm), lambda i: (0, i)), pl.BlockSpec((TOP_K, tm), lambda i: (0, i)),
                      pl.BlockSpec((1, N_EXPERTS, LANES), lambda i: (i, 0, 0))]
        out_shape += [jax.ShapeDtypeStruct((TOP_K, t), F32), jax.ShapeDtypeStruct((TOP_K, t), jnp.int32),
                      jax.ShapeDtypeStruct((t // tm, N_EXPERTS, LANES), jnp.int32)]
    else:
        wg, wu, wd, ffn_layer, g2, b2 = ffn
        const3 = lambda i: (ffn_layer, 0, 0)
        in_specs += [pl.BlockSpec((1, D_MODEL, D_FF), const3, **resident),
                     pl.BlockSpec((1, D_MODEL, D_FF), const3, **resident),
                     pl.BlockSpec((1, D_FF, D_MODEL), const3, **resident),
                     pl.BlockSpec((1, D_MODEL), const), pl.BlockSpec((1, D_MODEL), const)]
        args += [wg, wu, wd, g2, b2]
        scratch = [pltpu.VMEM((tm, D_FF), BF16)]
    return pl.pallas_call(
        functools.partial(_outproj_kernel, alpha=alpha, mode="route" if ffn is None else "ffn"),
        grid=(t // tm,),
        in_specs=in_specs,
        out_specs=out_specs,
        out_shape=out_shape,
        scratch_shapes=scratch,
        compiler_params=_params(vmem=V7X_VMEM_LIMIT_BYTES),
        name="out_proj_ln1" if ffn is None else "out_proj_ffn",
    )(*args)


def _slab_copies(n_rows, src, src_row, dst, dst_row, sem, action, max_rows):
    shift = SLAB_ALIGN.bit_length() - 1
    for b in reversed(range((max_rows // SLAB_ALIGN).bit_length())):
        size = SLAB_ALIGN << b
        if size > max_rows:
            continue
        done = (n_rows >> (shift + b + 1)) << (shift + b + 1)

        @pl.when(((n_rows >> (shift + b)) & 1) == 1)
        def _(size=size, done=done):
            s = pl.multiple_of(src_row + done, SLAB_ALIGN)
            d = pl.multiple_of(dst_row + done, SLAB_ALIGN)
            action(pltpu.make_async_copy(src.at[pl.ds(s, size)], dst.at[pl.ds(d, size)], sem))


def _start(copy):
    copy.start()


def _wait(copy):
    copy.wait()


def _dispatch_kernel(n_ref, off_ref, loff_ref, padoff_ref, padlen_ref, nvalid_ref, x_ref, lp_ref, xs_hbm,
                     slab, zeros, sems):
    i = pl.program_id(0)
    slot = i % 2

    @pl.when(i == 0)
    def _():
        zeros[...] = jnp.zeros_like(zeros)
        tile_rows = zeros.shape[0]
        sem = sems.at[2]
        for action in (_start, _wait):
            for e in range(N_EXPERTS):
                _slab_copies(padlen_ref[e], zeros, 0, xs_hbm, padoff_ref[e], sem, action, tile_rows)

            def unused_tile(j, carry, action=action):
                row = pl.multiple_of(j * tile_rows, tile_rows)
                action(pltpu.make_async_copy(zeros, xs_hbm.at[pl.ds(row, tile_rows)], sem))
                return carry
            lax.fori_loop(nvalid_ref[0], xs_hbm.shape[0] // tile_rows, unused_tile, 0)

    def copies(tile, sl, action):
        for e in range(N_EXPERTS):
            k = tile * N_EXPERTS + e
            _slab_copies(n_ref[k], slab.at[sl], loff_ref[k], xs_hbm, off_ref[k], sems.at[sl], action, MOE_TILE)

    lp = lp_ref[...]
    q = lax.broadcasted_iota(jnp.int32, (slab.shape[1], lp.shape[1]), 0)
    sel = jnp.where(q == lp[0:1, :], 1.0, jnp.where(q == lp[1:2, :], 1.0, 0.0))
    slab[slot] = jnp.dot(sel.astype(BF16), x_ref[...].astype(BF16), preferred_element_type=F32).astype(BF16)
    copies(i, slot, _start)

    @pl.when(i > 0)
    def _():
        copies(i - 1, 1 - slot, _wait)

    @pl.when(i == pl.num_programs(0) - 1)
    def _():
        copies(i, slot, _wait)


def _moe_dispatch(x, lp, route, n_rows):
    t = x.shape[0]
    tt = MOE_TILE
    imap2 = lambda i, *_: (i, 0)
    return pl.pallas_call(
        _dispatch_kernel,
        grid_spec=pltpu.PrefetchScalarGridSpec(
            num_scalar_prefetch=6,
            grid=(t // tt,),
            in_specs=[
                pl.BlockSpec((tt, D_MODEL), imap2),
                pl.BlockSpec((TOP_K, tt), lambda i, *_: (0, i)),
            ],
            out_specs=pl.BlockSpec(memory_space=pl.ANY),
            scratch_shapes=[
                pltpu.VMEM((2, SLAB_ROWS, D_MODEL), BF16),
                pltpu.VMEM((ROW_TILE, D_MODEL), BF16),
                pltpu.SemaphoreType.DMA((3,)),
            ],
        ),
        out_shape=jax.ShapeDtypeStruct((n_rows, D_MODEL), BF16),
        compiler_params=_params(vmem=V7X_VMEM_LIMIT_BYTES),
        name="moe_dispatch",
    )(route["n"], route["off"], route["loff"], route["pad_off"], route["pad_len"], route["nvalid"], x,
      lp)


def _moe_kernel(tile_e_ref, nvalid_ref, x_ref, wg_ref, wu_ref, wd_ref, y_ref, h_ref):
    del tile_e_ref
    j = pl.program_id(0)
    nv = nvalid_ref[0]

    @pl.when(j < nv)
    def _():
        y_ref[...] = _swiglu_tile(x_ref[...], wg_ref, wu_ref, wd_ref, h_ref).astype(BF16)

    @pl.when(j >= nv)
    def _():
        y_ref[...] = jnp.zeros_like(y_ref)


def _moe_experts(xs, tile_e, nvalid, wg, wu, wd, layer):
    tm = ROW_TILE
    n_tiles = xs.shape[0] // tm
    wspec = lambda shape: pl.BlockSpec((None, 1) + shape, lambda j, te, nv: (layer, te[j], 0, 0))
    return pl.pallas_call(
        _moe_kernel,
        grid_spec=pltpu.PrefetchScalarGridSpec(
            num_scalar_prefetch=2,
            grid=(n_tiles,),
            in_specs=[
                pl.BlockSpec((tm, D_MODEL), lambda j, te, nv: (jnp.minimum(j, nv[0] - 1), 0)),
                wspec((D_MODEL, D_FF)), wspec((D_MODEL, D_FF)), wspec((D_FF, D_MODEL)),
            ],
            out_specs=pl.BlockSpec((tm, D_MODEL), lambda j, te, nv: (j, 0)),
            scratch_shapes=[pltpu.VMEM((tm, D_FF), BF16)],
        ),
        out_shape=jax.ShapeDtypeStruct(xs.shape, BF16),
        compiler_params=_params(vmem=V7X_VMEM_LIMIT_BYTES),
        name="moe_experts",
    )(tile_e, nvalid, xs, wg, wu, wd)


def _combine_kernel(n_ref, off_ref, loff_ref, x_ref, lp_ref, w_ref, ys_hbm, g_ref, b_ref, *rest, alpha, split):
    out_refs, (yslab, sems) = rest[:-2], rest[-2:]
    i = pl.program_id(0)
    slot = i % 2

    def copies(tile, sl, action):
        for e in range(N_EXPERTS):
            k = tile * N_EXPERTS + e
            _slab_copies(n_ref[k], ys_hbm, off_ref[k], yslab.at[sl], loff_ref[k], sems.at[sl], action, MOE_TILE)

    @pl.when(i == 0)
    def _():
        yslab[...] = jnp.zeros_like(yslab)
        copies(0, 0, _start)

    @pl.when(i + 1 < pl.num_programs(0))
    def _():
        copies(i + 1, 1 - slot, _start)

    copies(i, slot, _wait)
    lp = lp_ref[...]
    w = w_ref[...]
    q = lax.broadcasted_iota(jnp.int32, (yslab.shape[1], lp.shape[1]), 0)
    gate_t = jnp.where(q == lp[0:1, :], w[0:1, :], 0.0) + jnp.where(q == lp[1:2, :], w[1:2, :], 0.0)
    f = lax.dot_general(gate_t.astype(BF16), yslab[slot], (((0,), (0,)), ((), ())),
                        preferred_element_type=F32)
    o = _layernorm(alpha * x_ref[...] + f, g_ref[...], b_ref[...])
    if split is None:
        out_refs[0][...] = o
    else:
        @pl.when(i < split)
        def _():
            out_refs[0][...] = o

        @pl.when(i >= split)
        def _():
            out_refs[1][...] = o


def _moe_combine(x, gate_w, lp, route, y_sorted, g, b, alpha, split_rows=None):
    t = x.shape[0]
    tt = MOE_TILE
    n = t // tt
    imap2 = lambda i, *_: (i, 0)
    const = lambda i, *_: (0, 0)
    if split_rows is None:
        split = None
        out_specs = pl.BlockSpec((tt, D_MODEL), imap2)
        out_shape = jax.ShapeDtypeStruct((t, D_MODEL), F32)
    else:
        assert split_rows % tt == 0 and 0 < split_rows < t
        split = split_rows // tt
        out_specs = [pl.BlockSpec((tt, D_MODEL), lambda i, *_: (jnp.minimum(i, split - 1), 0)),
                     pl.BlockSpec((tt, D_MODEL), lambda i, *_: (jnp.maximum(i - split, 0), 0))]
        out_shape = [jax.ShapeDtypeStruct((split_rows, D_MODEL), F32),
                     jax.ShapeDtypeStruct((t - split_rows, D_MODEL), F32)]
    return pl.pallas_call(
        functools.partial(_combine_kernel, alpha=alpha, split=split),
        grid_spec=pltpu.PrefetchScalarGridSpec(
            num_scalar_prefetch=3,
            grid=(n,),
            in_specs=[
                pl.BlockSpec((tt, D_MODEL), imap2),
                pl.BlockSpec((TOP_K, tt), lambda i, *_: (0, i)),
                pl.BlockSpec((TOP_K, tt), lambda i, *_: (0, i)),
                pl.BlockSpec(memory_space=pl.ANY),
                pl.BlockSpec((1, D_MODEL), const),
                pl.BlockSpec((1, D_MODEL), const),
            ],
            out_specs=out_specs,
            scratch_shapes=[pltpu.VMEM((2, SLAB_ROWS, D_MODEL), BF16), pltpu.SemaphoreType.DMA((2,))],
        ),
        out_shape=out_shape,
        compiler_params=_params(vmem=V7X_VMEM_LIMIT_BYTES),
        name="moe_combine_ln2",
    )(route["n"], route["off"], route["loff"], x, lp, gate_w, y_sorted, g, b)


def _route(n, t):
    i32 = jnp.int32
    loff = jnp.cumsum(n, axis=1) - n
    seg_rows = jnp.sum(n, axis=0)
    seg_len = ((seg_rows + ROW_TILE - 1) // ROW_TILE) * ROW_TILE
    ends = jnp.cumsum(seg_len)
    goff = ends - seg_len
    off = goff[None, :] + jnp.cumsum(n, axis=0) - n
    n_tiles = _sorted_rows(t) // ROW_TILE
    nvalid = ends[-1] // ROW_TILE
    starts = jnp.arange(n_tiles, dtype=i32) * ROW_TILE
    tile_e = jnp.minimum(jnp.sum((starts[:, None] >= ends[None, :]).astype(i32), axis=1), N_EXPERTS - 1)
    tile_e = jnp.where(jnp.arange(n_tiles) < nvalid, tile_e, tile_e[nvalid - 1])
    flat = lambda v: v.reshape(-1).astype(i32)
    return dict(n=flat(n), off=flat(off), loff=flat(loff), pad_off=flat(goff + seg_rows),
                pad_len=flat(seg_len - seg_rows), tile_e=flat(tile_e), nvalid=flat(nvalid))


def _sorted_rows(t):
    worst = t * TOP_K + (t // MOE_TILE) * N_EXPERTS * (SLAB_ALIGN - 1) + N_EXPERTS * (ROW_TILE - 1)
    return -(-worst // ROW_TILE) * ROW_TILE


def _moe_layer(x, gate_w, lp, slab_rows, wg, wu, wd, layer, g, b, alpha, split_rows=None):
    t = x.shape[0]
    route = _route(slab_rows[:, :, 0], t)
    xs = _moe_dispatch(x, lp, route, _sorted_rows(t))
    ys = _moe_experts(xs, route["tile_e"], route["nvalid"], wg, wu, wd, layer)
    return _moe_combine(x, gate_w, lp, route, ys, g, b, alpha, split_rows)


def _segment_tables(groups, unit):
    prev, nxt, within, per_seq = [], [], [], []
    base = 0
    for n_seq, seq_len in groups:
        nb = seq_len // unit
        for _ in range(n_seq):
            for n in range(nb):
                i = base + n
                prev.append(i - 1 if n > 0 else i)
                nxt.append(i + 1 if n < nb - 1 else i)
                within.append(n)
                per_seq.append(nb)
            base += nb
    as_i32 = lambda v: jnp.asarray(np.asarray(v, np.int32))
    return as_i32(prev), as_i32(nxt), as_i32(within), as_i32(per_seq)


def _window_tables(groups):
    prev, nxt, within, per_seq = _segment_tables(groups, ROW_TILE)
    per_tile = ROW_TILE // WBLK
    tile = jnp.arange(prev.shape[0], dtype=jnp.int32)
    first = (within == 0).astype(jnp.int32)
    last = (within == per_seq - 1).astype(jnp.int32)
    prev_blk = jnp.where(first == 1, tile * per_tile, tile * per_tile - 1)
    next_blk = jnp.where(last == 1, tile * per_tile + per_tile - 1, tile * per_tile + per_tile)
    i = np.arange(WBLK)[:, None]
    jj = np.arange(3 * WBLK)[None, :]
    band = np.abs(jj - WBLK - i) <= WINDOW
    masks = []
    for no_prev, no_next in ((True, False), (False, False), (False, True)):
        inr = np.ones_like(band)
        if no_prev:
            inr = inr & (jj >= WBLK)
        if no_next:
            inr = inr & (jj < 2 * WBLK)
        masks.append(np.where(band & inr, 0.0, NEG))
    return prev_blk, next_blk, first, last, jnp.asarray(np.stack(masks).astype(np.float32))


def _nbr_tables(groups):
    chunk = NA_CHUNK_ROWS * GRID_W
    _, _, within, per_seq = _segment_tables(groups, chunk)
    idx = jnp.arange(within.shape[0], dtype=jnp.int32)
    win_chunk = jnp.clip(within - 1, 0, per_seq - NA_WINDOW_CHUNKS)
    return idx - within + win_chunk, win_chunk * NA_CHUNK_ROWS, within * NA_CHUNK_ROWS, per_seq * NA_CHUNK_ROWS


def _nbr_bias_pairs(na_rpb):
    c = np.arange(GRID_W)[:, None]
    kc = np.arange(GRID_W)[None, :]
    cstart = np.clip(c - NA_COLS // 2, 0, GRID_W - NA_COLS)
    allowed = (kc >= cstart) & (kc < cstart + NA_COLS)
    dc = np.clip(kc - c + (NA_COLS - 1), 0, 2 * NA_COLS - 2)
    pick = jnp.asarray((dc[:, :, None] == np.arange(2 * NA_COLS - 1)).astype(np.float32))
    looked_up = jnp.einsum('lhrd,ckd->lhrck', na_rpb.astype(F32), pick, precision=lax.Precision.HIGHEST)
    full = jnp.where(jnp.asarray(allowed)[None, None, None], looked_up * LOG2E, NEG)
    pairs = jnp.concatenate([full[:, :, :-1], full[:, :, 1:]], axis=-1)
    nl, nh, nd = pairs.shape[:3]
    pairs = pairs.reshape(nl, nh // 2, 2, nd, GRID_W, 2 * GRID_W).transpose(0, 1, 3, 2, 4, 5)
    return pairs.reshape(nl, nh // 2, nd, 2 * GRID_W, 2 * GRID_W)


def _rope_tables(max_len):
    half = HEAD_DIM // 2
    inv = 1.0 / (ROPE_THETA ** (jnp.arange(half, dtype=F32) / half))
    ang = jnp.arange(max_len, dtype=F32)[:, None] * inv[None, :]
    cos, sin = jnp.cos(ang), jnp.sin(ang)
    reps = LANES // HEAD_DIM
    return jnp.tile(jnp.concatenate([cos, cos], axis=-1), (1, reps)), \
        jnp.tile(jnp.concatenate([-sin, sin], axis=-1), (1, reps))


def _trunk(xa, xb, groups, emb_ln_g, emb_ln_b, w_in, attn_sink, na_rpb, gnorm_a, gnorm_b, w_o, ln1_g, ln1_b,
           ffn_gate, ffn_up, ffn_down, router, exp_gate, exp_up, exp_down, ln2_g, ln2_b):
    depth = w_in.shape[0]
    alpha = (2.0 * depth) ** 0.25
    for n_seq, seq_len in groups:
        assert seq_len % (NA_CHUNK_ROWS * GRID_W) == 0 and seq_len // WBLK >= 2
        assert seq_len >= NA_WINDOW_CHUNKS * NA_CHUNK_ROWS * GRID_W
        assert (n_seq * seq_len) % ROW_TILE == 0 and seq_len % ROW_TILE == 0
        assert (n_seq * seq_len) % MOE_TILE == 0
    assert depth % 2 == 0, "the last layer must be a routed one: its combine kernel splits the outputs"
    row2 = lambda v: v.reshape(1, -1)

    win_tables = _window_tables(groups)
    nbr_tables = _nbr_tables(groups)
    cos_t, sin_t = _rope_tables(max(s for _, s in groups))
    _, _, pos_within, _ = _segment_tables(groups, ROW_TILE)
    bias_pairs = _nbr_bias_pairs(na_rpb)
    def pair_heads(w, axis):
        shape = w.shape[:axis] + (HKV_A, GQA, HEAD_DIM) + w.shape[axis + 1:]
        return jnp.swapaxes(w.reshape(shape), axis, axis + 1).reshape(w.shape)

    w_in_b = jnp.concatenate([pair_heads(w_in[:, :, :QA_W], 2), w_in[:, :, QA_W:]], axis=2).astype(BF16)
    w_o_b = jnp.concatenate([pair_heads(w_o[:, :QA_W], 1), w_o[:, QA_W:]], axis=1).astype(BF16)
    gnorm_a = pair_heads(gnorm_a, 1)
    ffn_b = [w.astype(BF16) for w in (ffn_gate, ffn_up, ffn_down)]
    exp_b = [w.astype(BF16) for w in (exp_gate, exp_up, exp_down)]
    router_t = jnp.swapaxes(router, 1, 2)

    for l in range(depth):
        if l == 0:
            qa, ka, va, qb, kb, vb, x = _inproj(xa, w_in_b, l, cos_t, sin_t, pos_within,
                                                embed=(xb, row2(emb_ln_g), row2(emb_ln_b)))
        else:
            qa, ka, va, qb, kb, vb = _inproj(x, w_in_b, l, cos_t, sin_t, pos_within)
        oa = _window_attn(qa, ka, va, attn_sink[l], row2(gnorm_a[l]), win_tables)
        ob = _nbr_attn(qb, kb, vb, bias_pairs[l], row2(gnorm_b[l]), nbr_tables)
        i = l // 2
        if l % 2 == 0:
            x, = _outproj(x, oa, ob, w_o_b, l, row2(ln1_g[l]), row2(ln1_b[l]), alpha,
                          ffn=(*ffn_b, i, row2(ln2_g[l]), row2(ln2_b[l])))
        else:
            x, gate_w, lp, slab_rows = _outproj(x, oa, ob, w_o_b, l, row2(ln1_g[l]), row2(ln1_b[l]), alpha,
                                                router_t=router_t[i])
            x = _moe_layer(x, gate_w, lp, slab_rows, *exp_b, i, row2(ln2_g[l]), row2(ln2_b[l]), alpha,
                           split_rows=xa.shape[0] if l == depth - 1 else None)
    return x


def kernel(x_prompt, x_sample, emb_ln_g, emb_ln_b, w_in, attn_sink, na_rpb, gnorm_a, gnorm_b, w_o, ln1_g, ln1_b,
           ffn_gate, ffn_up, ffn_down, router, exp_gate, exp_up, exp_down, ln2_g, ln2_b):
    groups = (x_prompt.shape[:2], x_sample.shape[:2])
    ta = x_prompt.shape[0] * x_prompt.shape[1]
    ya, yb = _trunk(x_prompt.reshape(ta, D_MODEL), x_sample.reshape(-1, D_MODEL), groups,
                    emb_ln_g, emb_ln_b, w_in, attn_sink, na_rpb, gnorm_a, gnorm_b, w_o, ln1_g, ln1_b,
                    ffn_gate, ffn_up, ffn_down, router, exp_gate, exp_up, exp_down, ln2_g, ln2_b)
    return ya.reshape(x_prompt.shape), yb.reshape(x_sample.shape)
```

```python
import functools

import numpy as np
import jax
import jax.numpy as jnp
from jax import lax
from jax.experimental import pallas as pl
from jax.experimental.pallas import tpu as pltpu

D_MODEL = 1024
HEAD_DIM = 64
H_A = 8
HKV_A = 2
GQA = H_A // HKV_A
H_B = 8
WINDOW = 128
WBLK = 128
ROPE_THETA = 10000.0
GRID_W = 64
NA_ROWS = 8
NA_COLS = 16
D_FF = 2816
N_EXPERTS = 8
TOP_K = 2
LN_EPS = 1e-5
NEG = -1e30
LOG2E = 1.4426950408889634
QA_W = H_A * HEAD_DIM
KVA_W = HKV_A * HEAD_DIM
B_W = H_B * HEAD_DIM
IN_W = QA_W + 2 * KVA_W + 3 * B_W
ROPE_W = QA_W + KVA_W

V7X_VMEM_LIMIT_BYTES = 56 * 1024 * 1024
LANES = 128
MXU_N = 256
ROW_TILE = 512
NA_CHUNK_ROWS = 8
NA_WINDOW_CHUNKS = 3
MOE_TILE = 512
SLAB_ALIGN = 16
SLAB_ROWS = 1152

F32 = jnp.float32
BF16 = jnp.bfloat16


def _layernorm(y, g, b):
    mu = jnp.mean(y, axis=-1, keepdims=True)
    yc = y - mu
    var = jnp.mean(yc * yc, axis=-1, keepdims=True)
    return yc * lax.rsqrt(var + LN_EPS) * g + b


def _rmsnorm(o, g):
    ms = jnp.mean(o * o, axis=-1, keepdims=True)
    return o * lax.rsqrt(ms + LN_EPS) * g


def _dot_nt(a, b):
    return lax.dot_general(a, b, (((1,), (1,)), ((), ())), preferred_element_type=F32)


def _params(n_axes=1, vmem=None):
    return pltpu.CompilerParams(dimension_semantics=("arbitrary",) * n_axes, vmem_limit_bytes=vmem)


def _inproj_kernel(pos_ref, *refs, embed_tiles):
    del pos_ref
    if embed_tiles is None:
        x_ref, w_ref, cos_ref, sin_ref, qa_ref, ka_ref, va_ref, qb_ref, kb_ref, vb_ref = refs
        x = x_ref[...]
    else:
        (xa_ref, xb_ref, eg_ref, eb_ref, w_ref, cos_ref, sin_ref,
         qa_ref, ka_ref, va_ref, qb_ref, kb_ref, vb_ref, xln_ref) = refs
        raw = jnp.where(pl.program_id(0) < embed_tiles, xa_ref[...], xb_ref[...])
        x = _layernorm(raw, eg_ref[...], eb_ref[...])
        xln_ref[...] = x
    xb = x.astype(BF16)
    cos = cos_ref[...]
    sin = sin_ref[...]
    lane = lax.broadcasted_iota(jnp.int32, cos.shape, 1)
    first_half = (lane % HEAD_DIM) < (HEAD_DIM // 2)
    scale = HEAD_DIM ** -0.5 * LOG2E

    def rope(h):
        partner = jnp.where(first_half, pltpu.roll(h, LANES - HEAD_DIM // 2, 1),
                            pltpu.roll(h, HEAD_DIM // 2, 1))
        return h * cos + partner * sin

    for c in range(QA_W // MXU_N):
        h = jnp.dot(xb, w_ref[:, c * MXU_N:(c + 1) * MXU_N], preferred_element_type=F32)
        for half in range(MXU_N // LANES):
            lo = c * MXU_N + half * LANES
            qa_ref[:, lo:lo + LANES] = (rope(h[:, half * LANES:(half + 1) * LANES]) * scale).astype(BF16)
    h = jnp.dot(xb, w_ref[:, QA_W:QA_W + 2 * KVA_W], preferred_element_type=F32)
    ka_ref[...] = rope(h[:, :KVA_W]).astype(BF16)
    va_ref[...] = h[:, KVA_W:].astype(BF16)
    off = QA_W + 2 * KVA_W
    for j, (ref, s) in enumerate(((qb_ref, scale), (kb_ref, 1.0), (vb_ref, 1.0))):
        for c in range(B_W // MXU_N):
            lo = off + j * B_W + c * MXU_N
            h = jnp.dot(xb, w_ref[:, lo:lo + MXU_N], preferred_element_type=F32)
            ref[:, c * MXU_N:(c + 1) * MXU_N] = (h * s).astype(BF16)


def _inproj(x, w_bf16, layer, cos_t, sin_t, pos_blk, embed=None):
    tm = ROW_TILE
    row = lambda i, p: (i, 0)
    const = lambda i, p: (0, 0)
    if embed is None:
        t = x.shape[0]
        embed_tiles = None
        x_specs = [pl.BlockSpec((tm, D_MODEL), row)]
        x_args = [x]
    else:
        xb, eg, eb = embed
        t = x.shape[0] + xb.shape[0]
        embed_tiles = x.shape[0] // tm
        x_specs = [pl.BlockSpec((tm, D_MODEL), lambda i, p: (jnp.minimum(i, embed_tiles - 1), 0)),
                   pl.BlockSpec((tm, D_MODEL), lambda i, p: (jnp.maximum(i - embed_tiles, 0), 0)),
                   pl.BlockSpec((1, D_MODEL), const), pl.BlockSpec((1, D_MODEL), const)]
        x_args = [x, xb, eg, eb]
    out_shapes = [jax.ShapeDtypeStruct((t, w), BF16) for w in (QA_W, KVA_W, KVA_W, B_W, B_W, B_W)]
    if embed is not None:
        out_shapes.append(jax.ShapeDtypeStruct((t, D_MODEL), F32))
    return pl.pallas_call(
        functools.partial(_inproj_kernel, embed_tiles=embed_tiles),
        grid_spec=pltpu.PrefetchScalarGridSpec(
            num_scalar_prefetch=1,
            grid=(t // tm,),
            in_specs=x_specs + [
                pl.BlockSpec((None, D_MODEL, IN_W), lambda i, p: (layer, 0, 0)),
                pl.BlockSpec((tm, LANES), lambda i, p: (p[i], 0)),
                pl.BlockSpec((tm, LANES), lambda i, p: (p[i], 0)),
            ],
            out_specs=[pl.BlockSpec((tm, s.shape[1]), row) for s in out_shapes],
        ),
        out_shape=out_shapes,
        compiler_params=_params(vmem=V7X_VMEM_LIMIT_BYTES),
        name="in_proj_rope",
    )(pos_blk, *x_args, w_bf16, cos_t, sin_t)


def _split_heads(qc):
    lo = lax.broadcasted_iota(jnp.int32, qc.shape, 1) < HEAD_DIM
    zero = jnp.zeros_like(qc)
    return jnp.concatenate([jnp.where(lo, qc, zero), jnp.where(lo, zero, qc)], axis=0)


def _merge_heads(o2):
    m = o2.shape[0] // 2
    lo = lax.broadcasted_iota(jnp.int32, (m, LANES), 1) < HEAD_DIM
    return jnp.where(lo, o2[:m], o2[m:])


def _win_kernel(wblk_ref, base_ref, first_ref, last_ref, sink_ref, q_ref, kwin, vwin, bias_ref, g_ref, o_ref):
    del wblk_ref
    i = pl.program_id(0)
    nsub = q_ref.shape[0] // WBLK
    first = first_ref[i]
    last = last_ref[i]
    base = base_ref[i]
    g = g_ref[...]
    ones = jnp.ones((3 * WBLK, LANES), BF16)

    def body(j, carry):
        off = pl.multiple_of(j * WBLK, WBLK)
        want = base + j * WBLK
        koff = pl.multiple_of(jnp.clip(want, 0, kwin.shape[0] - 3 * WBLK), WBLK)
        bt = jnp.where((j == 0) & (first == 1), jnp.where(want < koff, 3, 0),
                       jnp.where((j == nsub - 1) & (last == 1), jnp.where(want > koff, 4, 2), 1))
        bias = bias_ref[bt]
        k3 = kwin[pl.ds(koff, 3 * WBLK), :]
        v3 = jnp.concatenate([vwin[pl.ds(koff, 3 * WBLK), :], ones], axis=-1)
        q = q_ref[pl.ds(off, WBLK), :]
        outs = []
        for c in range(QA_W // LANES):
            qc = q[:, c * LANES:(c + 1) * LANES]
            lo = lax.broadcasted_iota(jnp.int32, qc.shape, 1) < HEAD_DIM
            halves = []
            for hh, qh in ((c, jnp.where(lo, qc, jnp.zeros_like(qc))), (GQA + c, jnp.where(lo, jnp.zeros_like(qc), qc))):
                s = _dot_nt(qh, k3) + bias
                sk = sink_ref[hh] * LOG2E
                m = jnp.maximum(jnp.max(s, axis=-1, keepdims=True), sk)
                p = jnp.exp2(s - m).astype(BF16)
                of = jnp.dot(p, v3, preferred_element_type=F32)
                halves.append(of[:, :LANES] / (of[:, LANES:] + jnp.exp2(sk - m)))
            outs.append(jnp.where(lo, halves[0], halves[1]))
        o = jnp.concatenate(outs, axis=-1)
        o_ref[pl.ds(off, WBLK), :] = _rmsnorm(o, g).astype(BF16)
        return carry

    lax.fori_loop(0, nsub, body, 0, unroll=True)


def _window_attn(qa, ka, va, sink, gnorm, tables):
    t = qa.shape[0]
    tq = ROW_TILE
    win_blk, base, first, last, wbias = tables
    win = pl.BlockSpec((pl.Element(tq + 2 * WBLK), pl.Element(KVA_W)),
                       lambda i, wb, ba, fi, la: (wb[i] * WBLK, 0))
    return pl.pallas_call(
        _win_kernel,
        grid_spec=pltpu.PrefetchScalarGridSpec(
            num_scalar_prefetch=4,
            grid=(t // tq,),
            in_specs=[
                pl.BlockSpec(memory_space=pltpu.SMEM),
                pl.BlockSpec((tq, QA_W), lambda i, wb, ba, fi, la: (i, 0)),
                win, win,
                pl.BlockSpec(wbias.shape, lambda i, wb, ba, fi, la: (0, 0, 0)),
                pl.BlockSpec((1, QA_W), lambda i, wb, ba, fi, la: (0, 0)),
            ],
            out_specs=pl.BlockSpec((tq, QA_W), lambda i, wb, ba, fi, la: (i, 0)),
        ),
        out_shape=jax.ShapeDtypeStruct((t, QA_W), BF16),
        compiler_params=_params(),
        name="window_attn",
    )(win_blk, base, first, last, sink, qa, ka, va, wbias, gnorm)


def _nbr_kernel(wtok_ref, wrow_ref, row0_ref, rows_ref, q_ref, kwin, vwin, bias_ref, g_ref, o_ref):
    del wtok_ref
    j = pl.program_id(0)
    row0 = row0_ref[j]
    rows = rows_ref[j]
    wrow = wrow_ref[j]
    g = g_ref[...]
    nkeys = NA_ROWS * GRID_W
    ones = jnp.ones((nkeys, LANES), BF16)

    def body(rr, carry):
        r = row0 + rr
        rstart = jnp.clip(r - NA_ROWS // 2, 0, rows - NA_ROWS)
        bidx = (NA_ROWS - 1) - (r - rstart)
        loc = pl.multiple_of((rstart - wrow) * GRID_W, GRID_W)
        qoff = pl.multiple_of(rr * GRID_W, GRID_W)
        q = q_ref[pl.ds(qoff, GRID_W), :]
        outs = []
        for c in range(B_W // LANES):
            kc = kwin[pl.ds(loc, nkeys), c * LANES:(c + 1) * LANES]
            vc = jnp.concatenate([vwin[pl.ds(loc, nkeys), c * LANES:(c + 1) * LANES], ones], axis=-1)
            s = _dot_nt(_split_heads(q[:, c * LANES:(c + 1) * LANES]), kc)
            s = s + jnp.concatenate([bias_ref[c, bidx + 2 * jj] for jj in range(NA_ROWS // 2)], axis=-1)
            m = jnp.max(s, axis=-1, keepdims=True)
            p = jnp.exp2(s - m).astype(BF16)
            of = jnp.dot(p, vc, preferred_element_type=F32)
            outs.append(_merge_heads(of[:, :LANES] / of[:, LANES:]))
        o = jnp.concatenate(outs, axis=-1)
        o_ref[pl.ds(qoff, GRID_W), :] = _rmsnorm(o, g).astype(BF16)
        return carry

    lax.fori_loop(0, NA_CHUNK_ROWS, body, 0, unroll=True)


def _nbr_attn(qb, kb, vb, bias_pairs, gnorm, tables):
    t = qb.shape[0]
    win_tok, win_row, row0, rows = tables
    chunk = NA_CHUNK_ROWS * GRID_W
    cur = pl.BlockSpec((chunk, B_W), lambda i, wt, wr, r0, rs: (i, 0))
    win = pl.BlockSpec((pl.Element(NA_WINDOW_CHUNKS * chunk), pl.Element(B_W)),
                       lambda i, wt, wr, r0, rs: (wt[i] * chunk, 0))
    return pl.pallas_call(
        _nbr_kernel,
        grid_spec=pltpu.PrefetchScalarGridSpec(
            num_scalar_prefetch=4,
            grid=(t // chunk,),
            in_specs=[
                cur, win, win,
                pl.BlockSpec(bias_pairs.shape, lambda i, wt, wr, r0, rs: (0, 0, 0, 0)),
                pl.BlockSpec((1, B_W), lambda i, wt, wr, r0, rs: (0, 0)),
            ],
            out_specs=cur,
        ),
        out_shape=jax.ShapeDtypeStruct((t, B_W), BF16),
        compiler_params=_params(vmem=V7X_VMEM_LIMIT_BYTES),
        name="nbr_attn",
    )(win_tok, win_row, row0, rows, qb, kb, vb, bias_pairs, gnorm)


def _route_tile(xn, rt_ref, w_ref, lp_ref, cnt_ref):
    tile = xn.shape[0]
    rt = rt_ref[...]
    rt_hi = rt.astype(BF16)
    rt_lo = (rt - rt_hi.astype(F32)).astype(BF16)
    xn_hi = xn.astype(BF16)
    xn_lo = (xn - xn_hi.astype(F32)).astype(BF16)
    part = _dot_nt(jnp.concatenate([rt_hi, rt_lo], axis=0), xn_hi)
    logits = part[:N_EXPERTS] + part[N_EXPERTS:] + _dot_nt(rt_hi, xn_lo)
    e = lax.broadcasted_iota(jnp.int32, logits.shape, 0)
    m1 = jnp.max(logits, axis=0, keepdims=True)
    i1 = jnp.min(jnp.where(logits == m1, e, N_EXPERTS), axis=0, keepdims=True)
    rest = jnp.where(e == i1, -jnp.inf, logits)
    m2 = jnp.max(rest, axis=0, keepdims=True)
    i2 = jnp.min(jnp.where(rest == m2, e, N_EXPERTS), axis=0, keepdims=True)
    t2 = jnp.exp(m2 - m1)
    w_ref[...] = jnp.concatenate([1.0 / (1.0 + t2), t2 / (1.0 + t2)], axis=0)
    oh1 = jnp.where(e == i1, 1.0, 0.0)
    oh2 = jnp.where(e == i2, 1.0, 0.0)
    both = oh1 + oh2
    earlier = jnp.where(lax.broadcasted_iota(jnp.int32, (tile, tile), 0)
                        < lax.broadcasted_iota(jnp.int32, (tile, tile), 1), 1.0, 0.0).astype(BF16)
    before = jnp.dot(both.astype(BF16), earlier, preferred_element_type=F32)
    count = jnp.sum(both, axis=1, keepdims=True).astype(jnp.int32)
    rows = ((count + (SLAB_ALIGN - 1)) // SLAB_ALIGN) * SLAB_ALIGN
    erow = lax.broadcasted_iota(jnp.int32, rows.shape, 0)
    start = jnp.zeros_like(rows)
    for k in range(N_EXPERTS - 1):
        start = start + jnp.where(erow > k, rows[k:k + 1, :], 0)
    where = start.astype(F32) + before
    lp_ref[...] = jnp.concatenate([jnp.sum(oh1 * where, axis=0, keepdims=True),
                                   jnp.sum(oh2 * where, axis=0, keepdims=True)], axis=0).astype(jnp.int32)
    cnt_ref[0] = jnp.broadcast_to(rows, cnt_ref.shape[1:])


def _swiglu_tile(xb, wg_ref, wu_ref, wd_ref, h_ref):
    for c in range(D_FF // MXU_N):
        sl = slice(c * MXU_N, (c + 1) * MXU_N)
        gate = jnp.dot(xb, wg_ref[0, :, sl], preferred_element_type=F32)
        up = jnp.dot(xb, wu_ref[0, :, sl], preferred_element_type=F32)
        h_ref[:, sl] = (gate / (1.0 + jnp.exp(-gate)) * up).astype(BF16)
    return jnp.dot(h_ref[...], wd_ref[0], preferred_element_type=F32)


def _outproj_kernel(x_ref, oa_ref, ob_ref, wa_ref, wb_ref, g_ref, b_ref, *rest, alpha, mode):
    mix = jnp.dot(oa_ref[...], wa_ref[...], preferred_element_type=F32)
    mix = mix + jnp.dot(ob_ref[...], wb_ref[...], preferred_element_type=F32)
    xn = _layernorm(alpha * x_ref[...] + mix, g_ref[...], b_ref[...])
    if mode == "route":
        rt_ref, o_ref, w_ref, lp_ref, cnt_ref = rest
        _route_tile(xn, rt_ref, w_ref, lp_ref, cnt_ref)
        o_ref[...] = xn
    else:
        wg_ref, wu_ref, wd_ref, g2_ref, b2_ref, o_ref, h_ref = rest
        f = _swiglu_tile(xn.astype(BF16), wg_ref, wu_ref, wd_ref, h_ref)
        o_ref[...] = _layernorm(alpha * xn + f, g2_ref[...], b2_ref[...])


def _outproj(x, oa, ob, w_o, layer, g, b, alpha, router_t=None, ffn=None):
    t = x.shape[0]
    tm = ROW_TILE
    assert tm == MOE_TILE and (router_t is None) != (ffn is None)
    row = lambda i: (i, 0)
    const = lambda i: (0, 0)
    resident = dict(pipeline_mode=pl.Buffered(1))
    in_specs = [
        pl.BlockSpec((tm, D_MODEL), row),
        pl.BlockSpec((tm, QA_W), row),
        pl.BlockSpec((tm, B_W), row),
        pl.BlockSpec((None, QA_W, D_MODEL), lambda i: (layer, 0, 0), **resident),
        pl.BlockSpec((None, B_W, D_MODEL), lambda i: (layer, QA_W // B_W, 0), **resident),
        pl.BlockSpec((1, D_MODEL), const),
        pl.BlockSpec((1, D_MODEL), const),
    ]
    out_specs = [pl.BlockSpec((tm, D_MODEL), row)]
    out_shape = [jax.ShapeDtypeStruct((t, D_MODEL), F32)]
    args = [x, oa, ob, w_o, w_o, g, b]
    scratch = []
    if router_t is not None:
        in_specs.append(pl.BlockSpec((N_EXPERTS, D_MODEL), const))
        args.append(router_t)
        out_specs += [pl.BlockSpec((TOP_K, tm), lambda i: (0, i)), pl.BlockSpec((TOP_K, tm), lambda i: (0, i)),
                      pl.BlockSpec((1, N_EXPERTS, LANES), lambda i: (i, 0, 0))]
        out_shape += [jax.ShapeDtypeStruct((TOP_K, t), F32), jax.ShapeDtypeStruct((TOP_K, t), jnp.int32),
                      jax.ShapeDtypeStruct((t // tm, N_EXPERTS, LANES), jnp.int32)]
    else:
        wg, wu, wd, ffn_layer, g2, b2 = ffn
        const3 = lambda i: (ffn_layer, 0, 0)
        in_specs += [pl.BlockSpec((1, D_MODEL, D_FF), const3, **resident),
                     pl.BlockSpec((1, D_MODEL, D_FF), const3, **resident),
                     pl.BlockSpec((1, D_FF, D_MODEL), const3, **resident),
                     pl.BlockSpec((1, D_MODEL), const), pl.BlockSpec((1, D_MODEL), const)]
        args += [wg, wu, wd, g2, b2]
        scratch = [pltpu.VMEM((tm, D_FF), BF16)]
    return pl.pallas_call(
        functools.partial(_outproj_kernel, alpha=alpha, mode="route" if ffn is None else "ffn"),
        grid=(t // tm,),
        in_specs=in_specs,
        out_specs=out_specs,
        out_shape=out_shape,
        scratch_shapes=scratch,
        compiler_params=_params(vmem=V7X_VMEM_LIMIT_BYTES),
        name="out_proj_ln1" if ffn is None else "out_proj_ffn",
    )(*args)


def _slab_copies(n_rows, src, src_row, dst, dst_row, sem, action, max_rows):
    shift = SLAB_ALIGN.bit_length() - 1
    for b in reversed(range((max_rows // SLAB_ALIGN).bit_length())):
        size = SLAB_ALIGN << b
        if size > max_rows:
            continue
        done = (n_rows >> (shift + b + 1)) << (shift + b + 1)

        @pl.when(((n_rows >> (shift + b)) & 1) == 1)
        def _(size=size, done=done):
            s = pl.multiple_of(src_row + done, SLAB_ALIGN)
            d = pl.multiple_of(dst_row + done, SLAB_ALIGN)
            action(pltpu.make_async_copy(src.at[pl.ds(s, size)], dst.at[pl.ds(d, size)], sem))


def _start(copy):
    copy.start()


def _wait(copy):
    copy.wait()


def _dispatch_kernel(n_ref, off_ref, loff_ref, padoff_ref, padlen_ref, nvalid_ref, x_ref, lp_ref, xs_hbm,
                     slab, zeros, sems):
    i = pl.program_id(0)
    slot = i % 2

    @pl.when(i == 0)
    def _():
        zeros[...] = jnp.zeros_like(zeros)
        tile_rows = zeros.shape[0]
        sem = sems.at[2]
        for action in (_start, _wait):
            for e in range(N_EXPERTS):
                _slab_copies(padlen_ref[e], zeros, 0, xs_hbm, padoff_ref[e], sem, action, tile_rows)

            def unused_tile(j, carry, action=action):
                row = pl.multiple_of(j * tile_rows, tile_rows)
                action(pltpu.make_async_copy(zeros, xs_hbm.at[pl.ds(row, tile_rows)], sem))
                return carry
            lax.fori_loop(nvalid_ref[0], xs_hbm.shape[0] // tile_rows, unused_tile, 0)

    def copies(tile, sl, action):
        for e in range(N_EXPERTS):
            k = tile * N_EXPERTS + e
            _slab_copies(n_ref[k], slab.at[sl], loff_ref[k], xs_hbm, off_ref[k], sems.at[sl], action, MOE_TILE)

    lp = lp_ref[...]
    q = lax.broadcasted_iota(jnp.int32, (slab.shape[1], lp.shape[1]), 0)
    sel = jnp.where(q == lp[0:1, :], 1.0, jnp.where(q == lp[1:2, :], 1.0, 0.0))
    slab[slot] = jnp.dot(sel.astype(BF16), x_ref[...].astype(BF16), preferred_element_type=F32).astype(BF16)
    copies(i, slot, _start)

    @pl.when(i > 0)
    def _():
        copies(i - 1, 1 - slot, _wait)

    @pl.when(i == pl.num_programs(0) - 1)
    def _():
        copies(i, slot, _wait)


def _moe_dispatch(x, lp, route, n_rows):
    t = x.shape[0]
    tt = MOE_TILE
    imap2 = lambda i, *_: (i, 0)
    return pl.pallas_call(
        _dispatch_kernel,
        grid_spec=pltpu.PrefetchScalarGridSpec(
            num_scalar_prefetch=6,
            grid=(t // tt,),
            in_specs=[
                pl.BlockSpec((tt, D_MODEL), imap2),
                pl.BlockSpec((TOP_K, tt), lambda i, *_: (0, i)),
            ],
            out_specs=pl.BlockSpec(memory_space=pl.ANY),
            scratch_shapes=[
                pltpu.VMEM((2, SLAB_ROWS, D_MODEL), BF16),
                pltpu.VMEM((ROW_TILE, D_MODEL), BF16),
                pltpu.SemaphoreType.DMA((3,)),
            ],
        ),
        out_shape=jax.ShapeDtypeStruct((n_rows, D_MODEL), BF16),
        compiler_params=_params(vmem=V7X_VMEM_LIMIT_BYTES),
        name="moe_dispatch",
    )(route["n"], route["off"], route["loff"], route["pad_off"], route["pad_len"], route["nvalid"], x,
      lp)


def _moe_kernel(tile_e_ref, nvalid_ref, x_ref, wg_ref, wu_ref, wd_ref, y_ref, h_ref):
    del tile_e_ref
    j = pl.program_id(0)
    nv = nvalid_ref[0]

    @pl.when(j < nv)
    def _():
        y_ref[...] = _swiglu_tile(x_ref[...], wg_ref, wu_ref, wd_ref, h_ref).astype(BF16)

    @pl.when(j >= nv)
    def _():
        y_ref[...] = jnp.zeros_like(y_ref)


def _moe_experts(xs, tile_e, nvalid, wg, wu, wd, layer):
    tm = ROW_TILE
    n_tiles = xs.shape[0] // tm
    wspec = lambda shape: pl.BlockSpec((None, 1) + shape, lambda j, te, nv: (layer, te[j], 0, 0))
    return pl.pallas_call(
        _moe_kernel,
        grid_spec=pltpu.PrefetchScalarGridSpec(
            num_scalar_prefetch=2,
            grid=(n_tiles,),
            in_specs=[
                pl.BlockSpec((tm, D_MODEL), lambda j, te, nv: (jnp.minimum(j, nv[0] - 1), 0)),
                wspec((D_MODEL, D_FF)), wspec((D_MODEL, D_FF)), wspec((D_FF, D_MODEL)),
            ],
            out_specs=pl.BlockSpec((tm, D_MODEL), lambda j, te, nv: (j, 0)),
            scratch_shapes=[pltpu.VMEM((tm, D_FF), BF16)],
        ),
        out_shape=jax.ShapeDtypeStruct(xs.shape, BF16),
        compiler_params=_params(vmem=V7X_VMEM_LIMIT_BYTES),
        name="moe_experts",
    )(tile_e, nvalid, xs, wg, wu, wd)


def _combine_kernel(n_ref, off_ref, loff_ref, x_ref, lp_ref, w_ref, ys_hbm, g_ref, b_ref, *rest, alpha, split):
    out_refs, (yslab, sems) = rest[:-2], rest[-2:]
    i = pl.program_id(0)
    slot = i % 2

    def copies(tile, sl, action):
        for e in range(N_EXPERTS):
            k = tile * N_EXPERTS + e
            _slab_copies(n_ref[k], ys_hbm, off_ref[k], yslab.at[sl], loff_ref[k], sems.at[sl], action, MOE_TILE)

    @pl.when(i == 0)
    def _():
        yslab[...] = jnp.zeros_like(yslab)
        copies(0, 0, _start)

    @pl.when(i + 1 < pl.num_programs(0))
    def _():
        copies(i + 1, 1 - slot, _start)

    copies(i, slot, _wait)
    lp = lp_ref[...]
    w = w_ref[...]
    q = lax.broadcasted_iota(jnp.int32, (yslab.shape[1], lp.shape[1]), 0)
    gate_t = jnp.where(q == lp[0:1, :], w[0:1, :], 0.0) + jnp.where(q == lp[1:2, :], w[1:2, :], 0.0)
    f = lax.dot_general(gate_t.astype(BF16), yslab[slot], (((0,), (0,)), ((), ())),
                        preferred_element_type=F32)
    o = _layernorm(alpha * x_ref[...] + f, g_ref[...], b_ref[...])
    if split is None:
        out_refs[0][...] = o
    else:
        @pl.when(i < split)
        def _():
            out_refs[0][...] = o

        @pl.when(i >= split)
        def _():
            out_refs[1][...] = o


def _moe_combine(x, gate_w, lp, route, y_sorted, g, b, alpha, split_rows=None):
    t = x.shape[0]
    tt = MOE_TILE
    n = t // tt
    imap2 = lambda i, *_: (i, 0)
    const = lambda i, *_: (0, 0)
    if split_rows is None:
        split = None
        out_specs = pl.BlockSpec((tt, D_MODEL), imap2)
        out_shape = jax.ShapeDtypeStruct((t, D_MODEL), F32)
    else:
        assert split_rows % tt == 0 and 0 < split_rows < t
        split = split_rows // tt
        out_specs = [pl.BlockSpec((tt, D_MODEL), lambda i, *_: (jnp.minimum(i, split - 1), 0)),
                     pl.BlockSpec((tt, D_MODEL), lambda i, *_: (jnp.maximum(i - split, 0), 0))]
        out_shape = [jax.ShapeDtypeStruct((split_rows, D_MODEL), F32),
                     jax.ShapeDtypeStruct((t - split_rows, D_MODEL), F32)]
    return pl.pallas_call(
        functools.partial(_combine_kernel, alpha=alpha, split=split),
        grid_spec=pltpu.PrefetchScalarGridSpec(
            num_scalar_prefetch=3,
            grid=(n,),
            in_specs=[
                pl.BlockSpec((tt, D_MODEL), imap2),
                pl.BlockSpec((TOP_K, tt), lambda i, *_: (0, i)),
                pl.BlockSpec((TOP_K, tt), lambda i, *_: (0, i)),
                pl.BlockSpec(memory_space=pl.ANY),
                pl.BlockSpec((1, D_MODEL), const),
                pl.BlockSpec((1, D_MODEL), const),
            ],
            out_specs=out_specs,
            scratch_shapes=[pltpu.VMEM((2, SLAB_ROWS, D_MODEL), BF16), pltpu.SemaphoreType.DMA((2,))],
        ),
        out_shape=out_shape,
        compiler_params=_params(vmem=V7X_VMEM_LIMIT_BYTES),
        name="moe_combine_ln2",
    )(route["n"], route["off"], route["loff"], x, lp, gate_w, y_sorted, g, b)


def _route(n, t):
    i32 = jnp.int32
    loff = jnp.cumsum(n, axis=1) - n
    seg_rows = jnp.sum(n, axis=0)
    seg_len = ((seg_rows + ROW_TILE - 1) // ROW_TILE) * ROW_TILE
    ends = jnp.cumsum(seg_len)
    goff = ends - seg_len
    off = goff[None, :] + jnp.cumsum(n, axis=0) - n
    n_tiles = _sorted_rows(t) // ROW_TILE
    nvalid = ends[-1] // ROW_TILE
    starts = jnp.arange(n_tiles, dtype=i32) * ROW_TILE
    tile_e = jnp.minimum(jnp.sum((starts[:, None] >= ends[None, :]).astype(i32), axis=1), N_EXPERTS - 1)
    tile_e = jnp.where(jnp.arange(n_tiles) < nvalid, tile_e, tile_e[nvalid - 1])
    flat = lambda v: v.reshape(-1).astype(i32)
    return dict(n=flat(n), off=flat(off), loff=flat(loff), pad_off=flat(goff + seg_rows),
                pad_len=flat(seg_len - seg_rows), tile_e=flat(tile_e), nvalid=flat(nvalid))


def _sorted_rows(t):
    worst = t * TOP_K + (t // MOE_TILE) * N_EXPERTS * (SLAB_ALIGN - 1) + N_EXPERTS * (ROW_TILE - 1)
    return -(-worst // ROW_TILE) * ROW_TILE


def _moe_layer(x, gate_w, lp, slab_rows, wg, wu, wd, layer, g, b, alpha, split_rows=None):
    t = x.shape[0]
    route = _route(slab_rows[:, :, 0], t)
    xs = _moe_dispatch(x, lp, route, _sorted_rows(t))
    ys = _moe_experts(xs, route["tile_e"], route["nvalid"], wg, wu, wd, layer)
    return _moe_combine(x, gate_w, lp, route, ys, g, b, alpha, split_rows)


def _segment_tables(groups, unit):
    prev, nxt, within, per_seq = [], [], [], []
    base = 0
    for n_seq, seq_len in groups:
        nb = seq_len // unit
        for _ in range(n_seq):
            for n in range(nb):
                i = base + n
                prev.append(i - 1 if n > 0 else i)
                nxt.append(i + 1 if n < nb - 1 else i)
                within.append(n)
                per_seq.append(nb)
            base += nb
    as_i32 = lambda v: jnp.asarray(np.asarray(v, np.int32))
    return as_i32(prev), as_i32(nxt), as_i32(within), as_i32(per_seq)


def _window_tables(groups):
    _, _, within, per_seq = _segment_tables(groups, ROW_TILE)
    per_tile = ROW_TILE // WBLK
    n_tiles = within.shape[0]
    tile = jnp.arange(n_tiles, dtype=jnp.int32)
    first = (within == 0).astype(jnp.int32)
    last = (within == per_seq - 1).astype(jnp.int32)
    win_blk = jnp.clip(tile * per_tile - 1, 0, n_tiles * per_tile - (per_tile + 2))
    base = (tile * per_tile - 1 - win_blk) * WBLK
    i = np.arange(WBLK)[:, None]
    jj = np.arange(3 * WBLK)[None, :]
    masks = []
    for shift, lo, hi in ((-WBLK, WBLK, 3 * WBLK), (-WBLK, 0, 3 * WBLK), (-WBLK, 0, 2 * WBLK),
                          (0, 0, 3 * WBLK), (-2 * WBLK, 0, 3 * WBLK)):
        ok = (np.abs(jj + shift - i) <= WINDOW) & (jj >= lo) & (jj < hi)
        masks.append(np.where(ok, 0.0, NEG))
    return win_blk, base, first, last, jnp.asarray(np.stack(masks).astype(np.float32))


def _nbr_tables(groups):
    chunk = NA_CHUNK_ROWS * GRID_W
    _, _, within, per_seq = _segment_tables(groups, chunk)
    idx = jnp.arange(within.shape[0], dtype=jnp.int32)
    win_chunk = jnp.clip(within - 1, 0, per_seq - NA_WINDOW_CHUNKS)
    return idx - within + win_chunk, win_chunk * NA_CHUNK_ROWS, within * NA_CHUNK_ROWS, per_seq * NA_CHUNK_ROWS


def _nbr_bias_pairs(na_rpb):
    c = np.arange(GRID_W)[:, None]
    kc = np.arange(GRID_W)[None, :]
    cstart = np.clip(c - NA_COLS // 2, 0, GRID_W - NA_COLS)
    allowed = (kc >= cstart) & (kc < cstart + NA_COLS)
    dc = np.clip(kc - c + (NA_COLS - 1), 0, 2 * NA_COLS - 2)
    pick = jnp.asarray((dc[:, :, None] == np.arange(2 * NA_COLS - 1)).astype(np.float32))
    looked_up = jnp.einsum('lhrd,ckd->lhrck', na_rpb.astype(F32), pick, precision=lax.Precision.HIGHEST)
    full = jnp.where(jnp.asarray(allowed)[None, None, None], looked_up * LOG2E, NEG)
    pairs = jnp.concatenate([full[:, :, :-1], full[:, :, 1:]], axis=-1)
    nl, nh, nd = pairs.shape[:3]
    pairs = pairs.reshape(nl, nh // 2, 2, nd, GRID_W, 2 * GRID_W).transpose(0, 1, 3, 2, 4, 5)
    return pairs.reshape(nl, nh // 2, nd, 2 * GRID_W, 2 * GRID_W)


def _rope_tables(max_len):
    half = HEAD_DIM // 2
    inv = 1.0 / (ROPE_THETA ** (jnp.arange(half, dtype=F32) / half))
    ang = jnp.arange(max_len, dtype=F32)[:, None] * inv[None, :]
    cos, sin = jnp.cos(ang), jnp.sin(ang)
    reps = LANES // HEAD_DIM
    return jnp.tile(jnp.concatenate([cos, cos], axis=-1), (1, reps)), \
        jnp.tile(jnp.concatenate([-sin, sin], axis=-1), (1, reps))


def _trunk(xa, xb, groups, emb_ln_g, emb_ln_b, w_in, attn_sink, na_rpb, gnorm_a, gnorm_b, w_o, ln1_g, ln1_b,
           ffn_gate, ffn_up, ffn_down, router, exp_gate, exp_up, exp_down, ln2_g, ln2_b):
    depth = w_in.shape[0]
    alpha = (2.0 * depth) ** 0.25
    for n_seq, seq_len in groups:
        assert seq_len % (NA_CHUNK_ROWS * GRID_W) == 0 and seq_len // WBLK >= 2
        assert seq_len >= NA_WINDOW_CHUNKS * NA_CHUNK_ROWS * GRID_W
        assert (n_seq * seq_len) % ROW_TILE == 0 and seq_len % ROW_TILE == 0
        assert (n_seq * seq_len) % MOE_TILE == 0
    assert depth % 2 == 0, "the last layer must be a routed one: its combine kernel splits the outputs"
    row2 = lambda v: v.reshape(1, -1)

    win_tables = _window_tables(groups)
    nbr_tables = _nbr_tables(groups)
    cos_t, sin_t = _rope_tables(max(s for _, s in groups))
    _, _, pos_within, _ = _segment_tables(groups, ROW_TILE)
    bias_pairs = _nbr_bias_pairs(na_rpb)
    def pair_heads(w, axis):
        shape = w.shape[:axis] + (HKV_A, GQA, HEAD_DIM) + w.shape[axis + 1:]
        return jnp.swapaxes(w.reshape(shape), axis, axis + 1).reshape(w.shape)

    w_in_b = jnp.concatenate([pair_heads(w_in[:, :, :QA_W], 2), w_in[:, :, QA_W:]], axis=2).astype(BF16)
    w_o_b = jnp.concatenate([pair_heads(w_o[:, :QA_W], 1), w_o[:, QA_W:]], axis=1).astype(BF16)
    gnorm_a = pair_heads(gnorm_a, 1)
    ffn_b = [w.astype(BF16) for w in (ffn_gate, ffn_up, ffn_down)]
    exp_b = [w.astype(BF16) for w in (exp_gate, exp_up, exp_down)]
    router_t = jnp.swapaxes(router, 1, 2)

    for l in range(depth):
        if l == 0:
            qa, ka, va, qb, kb, vb, x = _inproj(xa, w_in_b, l, cos_t, sin_t, pos_within,
                                                embed=(xb, row2(emb_ln_g), row2(emb_ln_b)))
        else:
            qa, ka, va, qb, kb, vb = _inproj(x, w_in_b, l, cos_t, sin_t, pos_within)
        oa = _window_attn(qa, ka, va, attn_sink[l], row2(gnorm_a[l]), win_tables)
        ob = _nbr_attn(qb, kb, vb, bias_pairs[l], row2(gnorm_b[l]), nbr_tables)
        i = l // 2
        if l % 2 == 0:
            x, = _outproj(x, oa, ob, w_o_b, l, row2(ln1_g[l]), row2(ln1_b[l]), alpha,
                          ffn=(*ffn_b, i, row2(ln2_g[l]), row2(ln2_b[l])))
        else:
            x, gate_w, lp, slab_rows = _outproj(x, oa, ob, w_o_b, l, row2(ln1_g[l]), row2(ln1_b[l]), alpha,
                                                router_t=router_t[i])
            x = _moe_layer(x, gate_w, lp, slab_rows, *exp_b, i, row2(ln2_g[l]), row2(ln2_b[l]), alpha,
                           split_rows=xa.shape[0] if l == depth - 1 else None)
    return x


def kernel(x_prompt, x_sample, emb_ln_g, emb_ln_b, w_in, attn_sink, na_rpb, gnorm_a, gnorm_b, w_o, ln1_g, ln1_b,
           ffn_gate, ffn_up, ffn_down, router, exp_gate, exp_up, exp_down, ln2_g, ln2_b):
    groups = (x_prompt.shape[:2], x_sample.shape[:2])
    ta = x_prompt.shape[0] * x_prompt.shape[1]
    ya, yb = _trunk(x_prompt.reshape(ta, D_MODEL), x_sample.reshape(-1, D_MODEL), groups,
                    emb_ln_g, emb_ln_b, w_in, attn_sink, na_rpb, gnorm_a, gnorm_b, w_o, ln1_g, ln1_b,
                    ffn_gate, ffn_up, ffn_down, router, exp_gate, exp_up, exp_down, ln2_g, ln2_b)
    return ya.reshape(x_prompt.shape), yb.reshape(x_sample.shape)
```

```python
import functools

import numpy as np
import jax
import jax.numpy as jnp
from jax import lax
from jax.experimental import pallas as pl
from jax.experimental.pallas import tpu as pltpu

D_MODEL = 1024
HEAD_DIM = 64
H_A = 8
HKV_A = 2
GQA = H_A // HKV_A
H_B = 8
WINDOW = 128
WBLK = 128
ROPE_THETA = 10000.0
GRID_W = 64
NA_ROWS = 8
NA_COLS = 16
D_FF = 2816
N_EXPERTS = 8
TOP_K = 2
LN_EPS = 1e-5
NEG = -1e30
LOG2E = 1.4426950408889634
QA_W = H_A * HEAD_DIM
KVA_W = HKV_A * HEAD_DIM
B_W = H_B * HEAD_DIM
IN_W = QA_W + 2 * KVA_W + 3 * B_W
ROPE_W = QA_W + KVA_W

V7X_VMEM_LIMIT_BYTES = 56 * 1024 * 1024
LANES = 128
MXU_N = 256
ROW_TILE = 512
NA_CHUNK_ROWS = 8
NA_WINDOW_CHUNKS = 3
DENSE_ROW_TILE = 1024
MOE_TILE = 512
SLAB_ALIGN = 16
SLAB_ROWS = 1152

F32 = jnp.float32
BF16 = jnp.bfloat16


def _layernorm(y, g, b):
    mu = jnp.mean(y, axis=-1, keepdims=True)
    yc = y - mu
    var = jnp.mean(yc * yc, axis=-1, keepdims=True)
    return yc * lax.rsqrt(var + LN_EPS) * g + b


def _rmsnorm(o, g):
    ms = jnp.mean(o * o, axis=-1, keepdims=True)
    return o * lax.rsqrt(ms + LN_EPS) * g


def _dot_nt(a, b):
    return lax.dot_general(a, b, (((1,), (1,)), ((), ())), preferred_element_type=F32)


def _params(n_axes=1, vmem=None):
    return pltpu.CompilerParams(dimension_semantics=("arbitrary",) * n_axes, vmem_limit_bytes=vmem)


def _inproj_kernel(pos_ref, *refs, embed_tiles):
    del pos_ref
    if embed_tiles is None:
        x_ref, w_ref, cos_ref, sin_ref, qa_ref, ka_ref, va_ref, qb_ref, kb_ref, vb_ref = refs
        x = x_ref[...]
    else:
        (xa_ref, xb_ref, eg_ref, eb_ref, w_ref, cos_ref, sin_ref,
         qa_ref, ka_ref, va_ref, qb_ref, kb_ref, vb_ref, xln_ref) = refs
        raw = jnp.where(pl.program_id(0) < embed_tiles, xa_ref[...], xb_ref[...])
        x = _layernorm(raw, eg_ref[...], eb_ref[...])
        xln_ref[...] = x
    xb = x.astype(BF16)
    cos = cos_ref[...]
    sin = sin_ref[...]
    lane = lax.broadcasted_iota(jnp.int32, cos.shape, 1)
    first_half = (lane % HEAD_DIM) < (HEAD_DIM // 2)
    scale = HEAD_DIM ** -0.5 * LOG2E

    def rope(h):
        partner = jnp.where(first_half, pltpu.roll(h, LANES - HEAD_DIM // 2, 1),
                            pltpu.roll(h, HEAD_DIM // 2, 1))
        return h * cos + partner * sin

    for c in range(QA_W // MXU_N):
        h = jnp.dot(xb, w_ref[:, c * MXU_N:(c + 1) * MXU_N], preferred_element_type=F32)
        for half in range(MXU_N // LANES):
            lo = c * MXU_N + half * LANES
            qa_ref[:, lo:lo + LANES] = (rope(h[:, half * LANES:(half + 1) * LANES]) * scale).astype(BF16)
    h = jnp.dot(xb, w_ref[:, QA_W:QA_W + 2 * KVA_W], preferred_element_type=F32)
    ka_ref[...] = rope(h[:, :KVA_W]).astype(BF16)
    va_ref[...] = h[:, KVA_W:].astype(BF16)
    off = QA_W + 2 * KVA_W
    for j, (ref, s) in enumerate(((qb_ref, scale), (kb_ref, 1.0), (vb_ref, 1.0))):
        for c in range(B_W // MXU_N):
            lo = off + j * B_W + c * MXU_N
            h = jnp.dot(xb, w_ref[:, lo:lo + MXU_N], preferred_element_type=F32)
            ref[:, c * MXU_N:(c + 1) * MXU_N] = (h * s).astype(BF16)


def _inproj(x, w_bf16, layer, cos_t, sin_t, pos_blk, embed=None):
    tm = ROW_TILE
    row = lambda i, p: (i, 0)
    const = lambda i, p: (0, 0)
    if embed is None:
        t = x.shape[0]
        embed_tiles = None
        x_specs = [pl.BlockSpec((tm, D_MODEL), row)]
        x_args = [x]
    else:
        xb, eg, eb = embed
        t = x.shape[0] + xb.shape[0]
        embed_tiles = x.shape[0] // tm
        x_specs = [pl.BlockSpec((tm, D_MODEL), lambda i, p: (jnp.minimum(i, embed_tiles - 1), 0)),
                   pl.BlockSpec((tm, D_MODEL), lambda i, p: (jnp.maximum(i - embed_tiles, 0), 0)),
                   pl.BlockSpec((1, D_MODEL), const), pl.BlockSpec((1, D_MODEL), const)]
        x_args = [x, xb, eg, eb]
    out_shapes = [jax.ShapeDtypeStruct((t, w), BF16) for w in (QA_W, KVA_W, KVA_W, B_W, B_W, B_W)]
    if embed is not None:
        out_shapes.append(jax.ShapeDtypeStruct((t, D_MODEL), F32))
    return pl.pallas_call(
        functools.partial(_inproj_kernel, embed_tiles=embed_tiles),
        grid_spec=pltpu.PrefetchScalarGridSpec(
            num_scalar_prefetch=1,
            grid=(t // tm,),
            in_specs=x_specs + [
                pl.BlockSpec((None, D_MODEL, IN_W), lambda i, p: (layer, 0, 0)),
                pl.BlockSpec((tm, LANES), lambda i, p: (p[i], 0)),
                pl.BlockSpec((tm, LANES), lambda i, p: (p[i], 0)),
            ],
            out_specs=[pl.BlockSpec((tm, s.shape[1]), row) for s in out_shapes],
        ),
        out_shape=out_shapes,
        compiler_params=_params(vmem=V7X_VMEM_LIMIT_BYTES),
        name="in_proj_rope",
    )(pos_blk, *x_args, w_bf16, cos_t, sin_t)


def _split_heads(qc):
    lo = lax.broadcasted_iota(jnp.int32, qc.shape, 1) < HEAD_DIM
    zero = jnp.zeros_like(qc)
    return jnp.concatenate([jnp.where(lo, qc, zero), jnp.where(lo, zero, qc)], axis=0)


def _merge_heads(o2):
    m = o2.shape[0] // 2
    lo = lax.broadcasted_iota(jnp.int32, (m, LANES), 1) < HEAD_DIM
    return jnp.where(lo, o2[:m], o2[m:])


def _win_kernel(wblk_ref, base_ref, first_ref, last_ref, sink_ref, q_ref, kwin, vwin, bias_ref, g_ref, o_ref):
    del wblk_ref
    i = pl.program_id(0)
    nsub = q_ref.shape[0] // WBLK
    first = first_ref[i]
    last = last_ref[i]
    base = base_ref[i]
    g = g_ref[...]
    ones = jnp.ones((3 * WBLK, LANES), BF16)

    def body(j, carry):
        off = pl.multiple_of(j * WBLK, WBLK)
        want = base + j * WBLK
        koff = pl.multiple_of(jnp.clip(want, 0, kwin.shape[0] - 3 * WBLK), WBLK)
        bt = jnp.where((j == 0) & (first == 1), jnp.where(want < koff, 3, 0),
                       jnp.where((j == nsub - 1) & (last == 1), jnp.where(want > koff, 4, 2), 1))
        bias = bias_ref[bt]
        k3 = kwin[pl.ds(koff, 3 * WBLK), :]
        v3 = jnp.concatenate([vwin[pl.ds(koff, 3 * WBLK), :], ones], axis=-1)
        q = q_ref[pl.ds(off, WBLK), :]
        outs = []
        for c in range(QA_W // LANES):
            qc = q[:, c * LANES:(c + 1) * LANES]
            lo = lax.broadcasted_iota(jnp.int32, qc.shape, 1) < HEAD_DIM
            halves = []
            for hh, qh in ((c, jnp.where(lo, qc, jnp.zeros_like(qc))), (GQA + c, jnp.where(lo, jnp.zeros_like(qc), qc))):
                s = _dot_nt(qh, k3) + bias
                sk = sink_ref[hh] * LOG2E
                m = jnp.maximum(jnp.max(s, axis=-1, keepdims=True), sk)
                p = jnp.exp2(s - m).astype(BF16)
                of = jnp.dot(p, v3, preferred_element_type=F32)
                halves.append(of[:, :LANES] / (of[:, LANES:] + jnp.exp2(sk - m)))
            outs.append(jnp.where(lo, halves[0], halves[1]))
        o = jnp.concatenate(outs, axis=-1)
        o_ref[pl.ds(off, WBLK), :] = _rmsnorm(o, g).astype(BF16)
        return carry

    lax.fori_loop(0, nsub, body, 0, unroll=True)


def _window_attn(qa, ka, va, sink, gnorm, tables):
    t = qa.shape[0]
    tq = ROW_TILE
    win_blk, base, first, last, wbias = tables
    win = pl.BlockSpec((pl.Element(tq + 2 * WBLK), pl.Element(KVA_W)),
                       lambda i, wb, ba, fi, la: (wb[i] * WBLK, 0))
    return pl.pallas_call(
        _win_kernel,
        grid_spec=pltpu.PrefetchScalarGridSpec(
            num_scalar_prefetch=4,
            grid=(t // tq,),
            in_specs=[
                pl.BlockSpec(memory_space=pltpu.SMEM),
                pl.BlockSpec((tq, QA_W), lambda i, wb, ba, fi, la: (i, 0)),
                win, win,
                pl.BlockSpec(wbias.shape, lambda i, wb, ba, fi, la: (0, 0, 0)),
                pl.BlockSpec((1, QA_W), lambda i, wb, ba, fi, la: (0, 0)),
            ],
            out_specs=pl.BlockSpec((tq, QA_W), lambda i, wb, ba, fi, la: (i, 0)),
        ),
        out_shape=jax.ShapeDtypeStruct((t, QA_W), BF16),
        compiler_params=_params(),
        name="window_attn",
    )(win_blk, base, first, last, sink, qa, ka, va, wbias, gnorm)


def _nbr_kernel(wtok_ref, wrow_ref, row0_ref, rows_ref, q_ref, kwin, vwin, bias_ref, g_ref, o_ref):
    del wtok_ref
    j = pl.program_id(0)
    row0 = row0_ref[j]
    rows = rows_ref[j]
    wrow = wrow_ref[j]
    g = g_ref[...]
    nkeys = NA_ROWS * GRID_W
    ones = jnp.ones((nkeys, LANES), BF16)

    def body(rr, carry):
        r = row0 + rr
        rstart = jnp.clip(r - NA_ROWS // 2, 0, rows - NA_ROWS)
        bidx = (NA_ROWS - 1) - (r - rstart)
        loc = pl.multiple_of((rstart - wrow) * GRID_W, GRID_W)
        qoff = pl.multiple_of(rr * GRID_W, GRID_W)
        q = q_ref[pl.ds(qoff, GRID_W), :]
        outs = []
        for c in range(B_W // LANES):
            kc = kwin[pl.ds(loc, nkeys), c * LANES:(c + 1) * LANES]
            vc = jnp.concatenate([vwin[pl.ds(loc, nkeys), c * LANES:(c + 1) * LANES], ones], axis=-1)
            s = _dot_nt(_split_heads(q[:, c * LANES:(c + 1) * LANES]), kc)
            s = s + jnp.concatenate([bias_ref[c, bidx + 2 * jj] for jj in range(NA_ROWS // 2)], axis=-1)
            m = jnp.max(s, axis=-1, keepdims=True)
            p = jnp.exp2(s - m).astype(BF16)
            of = jnp.dot(p, vc, preferred_element_type=F32)
            outs.append(_merge_heads(of[:, :LANES] / of[:, LANES:]))
        o = jnp.concatenate(outs, axis=-1)
        o_ref[pl.ds(qoff, GRID_W), :] = _rmsnorm(o, g).astype(BF16)
        return carry

    lax.fori_loop(0, NA_CHUNK_ROWS, body, 0, unroll=True)


def _nbr_attn(qb, kb, vb, bias_pairs, gnorm, tables):
    t = qb.shape[0]
    win_tok, win_row, row0, rows = tables
    chunk = NA_CHUNK_ROWS * GRID_W
    cur = pl.BlockSpec((chunk, B_W), lambda i, wt, wr, r0, rs: (i, 0))
    win = pl.BlockSpec((pl.Element(NA_WINDOW_CHUNKS * chunk), pl.Element(B_W)),
                       lambda i, wt, wr, r0, rs: (wt[i] * chunk, 0))
    return pl.pallas_call(
        _nbr_kernel,
        grid_spec=pltpu.PrefetchScalarGridSpec(
            num_scalar_prefetch=4,
            grid=(t // chunk,),
            in_specs=[
                cur, win, win,
                pl.BlockSpec(bias_pairs.shape, lambda i, wt, wr, r0, rs: (0, 0, 0, 0)),
                pl.BlockSpec((1, B_W), lambda i, wt, wr, r0, rs: (0, 0)),
            ],
            out_specs=cur,
        ),
        out_shape=jax.ShapeDtypeStruct((t, B_W), BF16),
        compiler_params=_params(vmem=V7X_VMEM_LIMIT_BYTES),
        name="nbr_attn",
    )(win_tok, win_row, row0, rows, qb, kb, vb, bias_pairs, gnorm)


def _route_tile(xn, rt_ref, w_ref, lp_ref, cnt_ref):
    tile = xn.shape[0]
    rt = rt_ref[...]
    rt_hi = rt.astype(BF16)
    rt_lo = (rt - rt_hi.astype(F32)).astype(BF16)
    xn_hi = xn.astype(BF16)
    xn_lo = (xn - xn_hi.astype(F32)).astype(BF16)
    part = _dot_nt(jnp.concatenate([rt_hi, rt_lo], axis=0), xn_hi)
    logits = part[:N_EXPERTS] + part[N_EXPERTS:] + _dot_nt(rt_hi, xn_lo)
    e = lax.broadcasted_iota(jnp.int32, logits.shape, 0)
    m1 = jnp.max(logits, axis=0, keepdims=True)
    i1 = jnp.min(jnp.where(logits == m1, e, N_EXPERTS), axis=0, keepdims=True)
    rest = jnp.where(e == i1, -jnp.inf, logits)
    m2 = jnp.max(rest, axis=0, keepdims=True)
    i2 = jnp.min(jnp.where(rest == m2, e, N_EXPERTS), axis=0, keepdims=True)
    t2 = jnp.exp(m2 - m1)
    w_ref[...] = jnp.concatenate([1.0 / (1.0 + t2), t2 / (1.0 + t2)], axis=0)
    oh1 = jnp.where(e == i1, 1.0, 0.0)
    oh2 = jnp.where(e == i2, 1.0, 0.0)
    both = oh1 + oh2
    earlier = jnp.where(lax.broadcasted_iota(jnp.int32, (tile, tile), 0)
                        < lax.broadcasted_iota(jnp.int32, (tile, tile), 1), 1.0, 0.0).astype(BF16)
    before = jnp.dot(both.astype(BF16), earlier, preferred_element_type=F32)
    count = jnp.sum(both, axis=1, keepdims=True).astype(jnp.int32)
    rows = ((count + (SLAB_ALIGN - 1)) // SLAB_ALIGN) * SLAB_ALIGN
    erow = lax.broadcasted_iota(jnp.int32, rows.shape, 0)
    start = jnp.zeros_like(rows)
    for k in range(N_EXPERTS - 1):
        start = start + jnp.where(erow > k, rows[k:k + 1, :], 0)
    where = start.astype(F32) + before
    lp_ref[...] = jnp.concatenate([jnp.sum(oh1 * where, axis=0, keepdims=True),
                                   jnp.sum(oh2 * where, axis=0, keepdims=True)], axis=0).astype(jnp.int32)
    cnt_ref[0] = jnp.broadcast_to(rows, cnt_ref.shape[1:])


def _swiglu_tile(xb, wg_ref, wu_ref, wd_ref, h_ref):
    for c in range(D_FF // MXU_N):
        sl = slice(c * MXU_N, (c + 1) * MXU_N)
        gate = jnp.dot(xb, wg_ref[0, :, sl], preferred_element_type=F32)
        up = jnp.dot(xb, wu_ref[0, :, sl], preferred_element_type=F32)
        h_ref[:, sl] = (gate / (1.0 + jnp.exp(-gate)) * up).astype(BF16)
    return jnp.dot(h_ref[...], wd_ref[0], preferred_element_type=F32)


def _outproj_kernel(x_ref, oa_ref, ob_ref, wa_ref, wb_ref, g_ref, b_ref, *rest, alpha, mode):
    mix = jnp.dot(oa_ref[...], wa_ref[...], preferred_element_type=F32)
    mix = mix + jnp.dot(ob_ref[...], wb_ref[...], preferred_element_type=F32)
    xn = _layernorm(alpha * x_ref[...] + mix, g_ref[...], b_ref[...])
    if mode == "route":
        rt_ref, o_ref, w_ref, lp_ref, cnt_ref = rest
        _route_tile(xn, rt_ref, w_ref, lp_ref, cnt_ref)
        o_ref[...] = xn
    else:
        wg_ref, wu_ref, wd_ref, g2_ref, b2_ref, o_ref, h_ref = rest
        f = _swiglu_tile(xn.astype(BF16), wg_ref, wu_ref, wd_ref, h_ref)
        o_ref[...] = _layernorm(alpha * xn + f, g2_ref[...], b2_ref[...])


def _outproj(x, oa, ob, w_o, layer, g, b, alpha, router_t=None, ffn=None):
    t = x.shape[0]
    assert ROW_TILE == MOE_TILE and (router_t is None) != (ffn is None)
    tm = ROW_TILE if ffn is None else DENSE_ROW_TILE
    assert t % tm == 0
    row = lambda i: (i, 0)
    const = lambda i: (0, 0)
    resident = dict(pipeline_mode=pl.Buffered(1))
    in_specs = [
        pl.BlockSpec((tm, D_MODEL), row),
        pl.BlockSpec((tm, QA_W), row),
        pl.BlockSpec((tm, B_W), row),
        pl.BlockSpec((None, QA_W, D_MODEL), lambda i: (layer, 0, 0), **resident),
        pl.BlockSpec((None, B_W, D_MODEL), lambda i: (layer, QA_W // B_W, 0), **resident),
        pl.BlockSpec((1, D_MODEL), const),
        pl.BlockSpec((1, D_MODEL), const),
    ]
    out_specs = [pl.BlockSpec((tm, D_MODEL), row)]
    out_shape = [jax.ShapeDtypeStruct((t, D_MODEL), F32)]
    args = [x, oa, ob, w_o, w_o, g, b]
    scratch = []
    if router_t is not None:
        in_specs.append(pl.BlockSpec((N_EXPERTS, D_MODEL), const))
        args.append(router_t)
        out_specs += [pl.BlockSpec((TOP_K, tm), lambda i: (0, i)), pl.BlockSpec((TOP_K, tm), lambda i: (0, i)),
                      pl.BlockSpec((1, N_EXPERTS, LANES), lambda i: (i, 0, 0))]
        out_shape += [jax.ShapeDtypeStruct((TOP_K, t), F32), jax.ShapeDtypeStruct((TOP_K, t), jnp.int32),
                      jax.ShapeDtypeStruct((t // tm, N_EXPERTS, LANES), jnp.int32)]
    else:
        wg, wu, wd, ffn_layer, g2, b2 = ffn
        const3 = lambda i: (ffn_layer, 0, 0)
        in_specs += [pl.BlockSpec((1, D_MODEL, D_FF), const3, **resident),
                     pl.BlockSpec((1, D_MODEL, D_FF), const3, **resident),
                     pl.BlockSpec((1, D_FF, D_MODEL), const3, **resident),
                     pl.BlockSpec((1, D_MODEL), const), pl.BlockSpec((1, D_MODEL), const)]
        args += [wg, wu, wd, g2, b2]
        scratch = [pltpu.VMEM((tm, D_FF), BF16)]
    return pl.pallas_call(
        functools.partial(_outproj_kernel, alpha=alpha, mode="route" if ffn is None else "ffn"),
        grid=(t // tm,),
        in_specs=in_specs,
        out_specs=out_specs,
        out_shape=out_shape,
        scratch_shapes=scratch,
        compiler_params=_params(vmem=V7X_VMEM_LIMIT_BYTES),
        name="out_proj_ln1" if ffn is None else "out_proj_ffn",
    )(*args)


def _slab_copies(n_rows, src, src_row, dst, dst_row, sem, action, max_rows):
    shift = SLAB_ALIGN.bit_length() - 1
    for b in reversed(range((max_rows // SLAB_ALIGN).bit_length())):
        size = SLAB_ALIGN << b
        if size > max_rows:
            continue
        done = (n_rows >> (shift + b + 1)) << (shift + b + 1)

        @pl.when(((n_rows >> (shift + b)) & 1) == 1)
        def _(size=size, done=done):
            s = pl.multiple_of(src_row + done, SLAB_ALIGN)
            d = pl.multiple_of(dst_row + done, SLAB_ALIGN)
            action(pltpu.make_async_copy(src.at[pl.ds(s, size)], dst.at[pl.ds(d, size)], sem))


def _start(copy):
    copy.start()


def _wait(copy):
    copy.wait()


def _dispatch_kernel(n_ref, off_ref, loff_ref, padoff_ref, padlen_ref, nvalid_ref, x_ref, lp_ref, xs_hbm,
                     slab, zeros, sems):
    i = pl.program_id(0)
    slot = i % 2

    @pl.when(i == 0)
    def _():
        zeros[...] = jnp.zeros_like(zeros)
        tile_rows = zeros.shape[0]
        sem = sems.at[2]
        for action in (_start, _wait):
            for e in range(N_EXPERTS):
                _slab_copies(padlen_ref[e], zeros, 0, xs_hbm, padoff_ref[e], sem, action, tile_rows)

            def unused_tile(j, carry, action=action):
                row = pl.multiple_of(j * tile_rows, tile_rows)
                action(pltpu.make_async_copy(zeros, xs_hbm.at[pl.ds(row, tile_rows)], sem))
                return carry
            lax.fori_loop(nvalid_ref[0], xs_hbm.shape[0] // tile_rows, unused_tile, 0)

    def copies(tile, sl, action):
        for e in range(N_EXPERTS):
            k = tile * N_EXPERTS + e
            _slab_copies(n_ref[k], slab.at[sl], loff_ref[k], xs_hbm, off_ref[k], sems.at[sl], action, MOE_TILE)

    lp = lp_ref[...]
    q = lax.broadcasted_iota(jnp.int32, (slab.shape[1], lp.shape[1]), 0)
    sel = jnp.where(q == lp[0:1, :], 1.0, jnp.where(q == lp[1:2, :], 1.0, 0.0))
    slab[slot] = jnp.dot(sel.astype(BF16), x_ref[...].astype(BF16), preferred_element_type=F32).astype(BF16)
    copies(i, slot, _start)

    @pl.when(i > 0)
    def _():
        copies(i - 1, 1 - slot, _wait)

    @pl.when(i == pl.num_programs(0) - 1)
    def _():
        copies(i, slot, _wait)


def _moe_dispatch(x, lp, route, n_rows):
    t = x.shape[0]
    tt = MOE_TILE
    imap2 = lambda i, *_: (i, 0)
    return pl.pallas_call(
        _dispatch_kernel,
        grid_spec=pltpu.PrefetchScalarGridSpec(
            num_scalar_prefetch=6,
            grid=(t // tt,),
            in_specs=[
                pl.BlockSpec((tt, D_MODEL), imap2),
                pl.BlockSpec((TOP_K, tt), lambda i, *_: (0, i)),
            ],
            out_specs=pl.BlockSpec(memory_space=pl.ANY),
            scratch_shapes=[
                pltpu.VMEM((2, SLAB_ROWS, D_MODEL), BF16),
                pltpu.VMEM((ROW_TILE, D_MODEL), BF16),
                pltpu.SemaphoreType.DMA((3,)),
            ],
        ),
        out_shape=jax.ShapeDtypeStruct((n_rows, D_MODEL), BF16),
        compiler_params=_params(vmem=V7X_VMEM_LIMIT_BYTES),
        name="moe_dispatch",
    )(route["n"], route["off"], route["loff"], route["pad_off"], route["pad_len"], route["nvalid"], x,
      lp)


def _moe_kernel(tile_e_ref, nvalid_ref, x_ref, wg_ref, wu_ref, wd_ref, y_ref, h_ref):
    del tile_e_ref
    j = pl.program_id(0)
    nv = nvalid_ref[0]

    @pl.when(j < nv)
    def _():
        y_ref[...] = _swiglu_tile(x_ref[...], wg_ref, wu_ref, wd_ref, h_ref).astype(BF16)

    @pl.when(j >= nv)
    def _():
        y_ref[...] = jnp.zeros_like(y_ref)


def _moe_experts(xs, tile_e, nvalid, wg, wu, wd, layer):
    tm = ROW_TILE
    n_tiles = xs.shape[0] // tm
    wspec = lambda shape: pl.BlockSpec((None, 1) + shape, lambda j, te, nv: (layer, te[j], 0, 0))
    return pl.pallas_call(
        _moe_kernel,
        grid_spec=pltpu.PrefetchScalarGridSpec(
            num_scalar_prefetch=2,
            grid=(n_tiles,),
            in_specs=[
                pl.BlockSpec((tm, D_MODEL), lambda j, te, nv: (jnp.minimum(j, nv[0] - 1), 0)),
                wspec((D_MODEL, D_FF)), wspec((D_MODEL, D_FF)), wspec((D_FF, D_MODEL)),
            ],
            out_specs=pl.BlockSpec((tm, D_MODEL), lambda j, te, nv: (j, 0)),
            scratch_shapes=[pltpu.VMEM((tm, D_FF), BF16)],
        ),
        out_shape=jax.ShapeDtypeStruct(xs.shape, BF16),
        compiler_params=_params(vmem=V7X_VMEM_LIMIT_BYTES),
        name="moe_experts",
    )(tile_e, nvalid, xs, wg, wu, wd)


def _combine_kernel(n_ref, off_ref, loff_ref, x_ref, lp_ref, w_ref, ys_hbm, g_ref, b_ref, *rest, alpha, split):
    out_refs, (yslab, sems) = rest[:-2], rest[-2:]
    i = pl.program_id(0)
    slot = i % 2

    def copies(tile, sl, action):
        for e in range(N_EXPERTS):
            k = tile * N_EXPERTS + e
            _slab_copies(n_ref[k], ys_hbm, off_ref[k], yslab.at[sl], loff_ref[k], sems.at[sl], action, MOE_TILE)

    @pl.when(i == 0)
    def _():
        yslab[...] = jnp.zeros_like(yslab)
        copies(0, 0, _start)

    @pl.when(i + 1 < pl.num_programs(0))
    def _():
        copies(i + 1, 1 - slot, _start)

    copies(i, slot, _wait)
    lp = lp_ref[...]
    w = w_ref[...]
    q = lax.broadcasted_iota(jnp.int32, (yslab.shape[1], lp.shape[1]), 0)
    gate_t = jnp.where(q == lp[0:1, :], w[0:1, :], 0.0) + jnp.where(q == lp[1:2, :], w[1:2, :], 0.0)
    f = lax.dot_general(gate_t.astype(BF16), yslab[slot], (((0,), (0,)), ((), ())),
                        preferred_element_type=F32)
    o = _layernorm(alpha * x_ref[...] + f, g_ref[...], b_ref[...])
    if split is None:
        out_refs[0][...] = o
    else:
        @pl.when(i < split)
        def _():
            out_refs[0][...] = o

        @pl.when(i >= split)
        def _():
            out_refs[1][...] = o


def _moe_combine(x, gate_w, lp, route, y_sorted, g, b, alpha, split_rows=None):
    t = x.shape[0]
    tt = MOE_TILE
    n = t // tt
    imap2 = lambda i, *_: (i, 0)
    const = lambda i, *_: (0, 0)
    if split_rows is None:
        split = None
        out_specs = pl.BlockSpec((tt, D_MODEL), imap2)
        out_shape = jax.ShapeDtypeStruct((t, D_MODEL), F32)
    else:
        assert split_rows % tt == 0 and 0 < split_rows < t
        split = split_rows // tt
        out_specs = [pl.BlockSpec((tt, D_MODEL), lambda i, *_: (jnp.minimum(i, split - 1), 0)),
                     pl.BlockSpec((tt, D_MODEL), lambda i, *_: (jnp.maximum(i - split, 0), 0))]
        out_shape = [jax.ShapeDtypeStruct((split_rows, D_MODEL), F32),
                     jax.ShapeDtypeStruct((t - split_rows, D_MODEL), F32)]
    return pl.pallas_call(
        functools.partial(_combine_kernel, alpha=alpha, split=split),
        grid_spec=pltpu.PrefetchScalarGridSpec(
            num_scalar_prefetch=3,
            grid=(n,),
            in_specs=[
                pl.BlockSpec((tt, D_MODEL), imap2),
                pl.BlockSpec((TOP_K, tt), lambda i, *_: (0, i)),
                pl.BlockSpec((TOP_K, tt), lambda i, *_: (0, i)),
                pl.BlockSpec(memory_space=pl.ANY),
                pl.BlockSpec((1, D_MODEL), const),
                pl.BlockSpec((1, D_MODEL), const),
            ],
            out_specs=out_specs,
            scratch_shapes=[pltpu.VMEM((2, SLAB_ROWS, D_MODEL), BF16), pltpu.SemaphoreType.DMA((2,))],
        ),
        out_shape=out_shape,
        compiler_params=_params(vmem=V7X_VMEM_LIMIT_BYTES),
        name="moe_combine_ln2",
    )(route["n"], route["off"], route["loff"], x, lp, gate_w, y_sorted, g, b)


def _route(n, t):
    i32 = jnp.int32
    loff = jnp.cumsum(n, axis=1) - n
    seg_rows = jnp.sum(n, axis=0)
    seg_len = ((seg_rows + ROW_TILE - 1) // ROW_TILE) * ROW_TILE
    ends = jnp.cumsum(seg_len)
    goff = ends - seg_len
    off = goff[None, :] + jnp.cumsum(n, axis=0) - n
    n_tiles = _sorted_rows(t) // ROW_TILE
    nvalid = ends[-1] // ROW_TILE
    starts = jnp.arange(n_tiles, dtype=i32) * ROW_TILE
    tile_e = jnp.minimum(jnp.sum((starts[:, None] >= ends[None, :]).astype(i32), axis=1), N_EXPERTS - 1)
    tile_e = jnp.where(jnp.arange(n_tiles) < nvalid, tile_e, tile_e[nvalid - 1])
    flat = lambda v: v.reshape(-1).astype(i32)
    return dict(n=flat(n), off=flat(off), loff=flat(loff), pad_off=flat(goff + seg_rows),
                pad_len=flat(seg_len - seg_rows), tile_e=flat(tile_e), nvalid=flat(nvalid))


def _sorted_rows(t):
    worst = t * TOP_K + (t // MOE_TILE) * N_EXPERTS * (SLAB_ALIGN - 1) + N_EXPERTS * (ROW_TILE - 1)
    return -(-worst // ROW_TILE) * ROW_TILE


def _moe_layer(x, gate_w, lp, slab_rows, wg, wu, wd, layer, g, b, alpha, split_rows=None):
    t = x.shape[0]
    route = _route(slab_rows[:, :, 0], t)
    xs = _moe_dispatch(x, lp, route, _sorted_rows(t))
    ys = _moe_experts(xs, route["tile_e"], route["nvalid"], wg, wu, wd, layer)
    return _moe_combine(x, gate_w, lp, route, ys, g, b, alpha, split_rows)


def _segment_tables(groups, unit):
    prev, nxt, within, per_seq = [], [], [], []
    base = 0
    for n_seq, seq_len in groups:
        nb = seq_len // unit
        for _ in range(n_seq):
            for n in range(nb):
                i = base + n
                prev.append(i - 1 if n > 0 else i)
                nxt.append(i + 1 if n < nb - 1 else i)
                within.append(n)
                per_seq.append(nb)
            base += nb
    as_i32 = lambda v: jnp.asarray(np.asarray(v, np.int32))
    return as_i32(prev), as_i32(nxt), as_i32(within), as_i32(per_seq)


def _window_tables(groups):
    _, _, within, per_seq = _segment_tables(groups, ROW_TILE)
    per_tile = ROW_TILE // WBLK
    n_tiles = within.shape[0]
    tile = jnp.arange(n_tiles, dtype=jnp.int32)
    first = (within == 0).astype(jnp.int32)
    last = (within == per_seq - 1).astype(jnp.int32)
    win_blk = jnp.clip(tile * per_tile - 1, 0, n_tiles * per_tile - (per_tile + 2))
    base = (tile * per_tile - 1 - win_blk) * WBLK
    i = np.arange(WBLK)[:, None]
    jj = np.arange(3 * WBLK)[None, :]
    masks = []
    for shift, lo, hi in ((-WBLK, WBLK, 3 * WBLK), (-WBLK, 0, 3 * WBLK), (-WBLK, 0, 2 * WBLK),
                          (0, 0, 3 * WBLK), (-2 * WBLK, 0, 3 * WBLK)):
        ok = (np.abs(jj + shift - i) <= WINDOW) & (jj >= lo) & (jj < hi)
        masks.append(np.where(ok, 0.0, NEG))
    return win_blk, base, first, last, jnp.asarray(np.stack(masks).astype(np.float32))


def _nbr_tables(groups):
    chunk = NA_CHUNK_ROWS * GRID_W
    _, _, within, per_seq = _segment_tables(groups, chunk)
    idx = jnp.arange(within.shape[0], dtype=jnp.int32)
    win_chunk = jnp.clip(within - 1, 0, per_seq - NA_WINDOW_CHUNKS)
    return idx - within + win_chunk, win_chunk * NA_CHUNK_ROWS, within * NA_CHUNK_ROWS, per_seq * NA_CHUNK_ROWS


def _nbr_bias_pairs(na_rpb):
    c = np.arange(GRID_W)[:, None]
    kc = np.arange(GRID_W)[None, :]
    cstart = np.clip(c - NA_COLS // 2, 0, GRID_W - NA_COLS)
    allowed = (kc >= cstart) & (kc < cstart + NA_COLS)
    dc = np.clip(kc - c + (NA_COLS - 1), 0, 2 * NA_COLS - 2)
    pick = jnp.asarray((dc[:, :, None] == np.arange(2 * NA_COLS - 1)).astype(np.float32))
    looked_up = jnp.einsum('lhrd,ckd->lhrck', na_rpb.astype(F32), pick, precision=lax.Precision.HIGHEST)
    full = jnp.where(jnp.asarray(allowed)[None, None, None], looked_up * LOG2E, NEG)
    pairs = jnp.concatenate([full[:, :, :-1], full[:, :, 1:]], axis=-1)
    nl, nh, nd = pairs.shape[:3]
    pairs = pairs.reshape(nl, nh // 2, 2, nd, GRID_W, 2 * GRID_W).transpose(0, 1, 3, 2, 4, 5)
    return pairs.reshape(nl, nh // 2, nd, 2 * GRID_W, 2 * GRID_W)


def _rope_tables(max_len):
    half = HEAD_DIM // 2
    inv = 1.0 / (ROPE_THETA ** (jnp.arange(half, dtype=F32) / half))
    ang = jnp.arange(max_len, dtype=F32)[:, None] * inv[None, :]
    cos, sin = jnp.cos(ang), jnp.sin(ang)
    reps = LANES // HEAD_DIM
    return jnp.tile(jnp.concatenate([cos, cos], axis=-1), (1, reps)), \
        jnp.tile(jnp.concatenate([-sin, sin], axis=-1), (1, reps))


def _trunk(xa, xb, groups, emb_ln_g, emb_ln_b, w_in, attn_sink, na_rpb, gnorm_a, gnorm_b, w_o, ln1_g, ln1_b,
           ffn_gate, ffn_up, ffn_down, router, exp_gate, exp_up, exp_down, ln2_g, ln2_b):
    depth = w_in.shape[0]
    alpha = (2.0 * depth) ** 0.25
    for n_seq, seq_len in groups:
        assert seq_len % (NA_CHUNK_ROWS * GRID_W) == 0 and seq_len // WBLK >= 2
        assert seq_len >= NA_WINDOW_CHUNKS * NA_CHUNK_ROWS * GRID_W
        assert (n_seq * seq_len) % ROW_TILE == 0 and seq_len % ROW_TILE == 0
        assert (n_seq * seq_len) % MOE_TILE == 0
    assert depth % 2 == 0, "the last layer must be a routed one: its combine kernel splits the outputs"
    row2 = lambda v: v.reshape(1, -1)

    win_tables = _window_tables(groups)
    nbr_tables = _nbr_tables(groups)
    cos_t, sin_t = _rope_tables(max(s for _, s in groups))
    _, _, pos_within, _ = _segment_tables(groups, ROW_TILE)
    bias_pairs = _nbr_bias_pairs(na_rpb)
    def pair_heads(w, axis):
        shape = w.shape[:axis] + (HKV_A, GQA, HEAD_DIM) + w.shape[axis + 1:]
        return jnp.swapaxes(w.reshape(shape), axis, axis + 1).reshape(w.shape)

    w_in_b = jnp.concatenate([pair_heads(w_in[:, :, :QA_W], 2), w_in[:, :, QA_W:]], axis=2).astype(BF16)
    w_o_b = jnp.concatenate([pair_heads(w_o[:, :QA_W], 1), w_o[:, QA_W:]], axis=1).astype(BF16)
    gnorm_a = pair_heads(gnorm_a, 1)
    ffn_b = [w.astype(BF16) for w in (ffn_gate, ffn_up, ffn_down)]
    exp_b = [w.astype(BF16) for w in (exp_gate, exp_up, exp_down)]
    router_t = jnp.swapaxes(router, 1, 2)

    for l in range(depth):
        if l == 0:
            qa, ka, va, qb, kb, vb, x = _inproj(xa, w_in_b, l, cos_t, sin_t, pos_within,
                                                embed=(xb, row2(emb_ln_g), row2(emb_ln_b)))
        else:
            qa, ka, va, qb, kb, vb = _inproj(x, w_in_b, l, cos_t, sin_t, pos_within)
        oa = _window_attn(qa, ka, va, attn_sink[l], row2(gnorm_a[l]), win_tables)
        ob = _nbr_attn(qb, kb, vb, bias_pairs[l], row2(gnorm_b[l]), nbr_tables)
        i = l // 2
        if l % 2 == 0:
            x, = _outproj(x, oa, ob, w_o_b, l, row2(ln1_g[l]), row2(ln1_b[l]), alpha,
                          ffn=(*ffn_b, i, row2(ln2_g[l]), row2(ln2_b[l])))
        else:
            x, gate_w, lp, slab_rows = _outproj(x, oa, ob, w_o_b, l, row2(ln1_g[l]), row2(ln1_b[l]), alpha,
                                                router_t=router_t[i])
            x = _moe_layer(x, gate_w, lp, slab_rows, *exp_b, i, row2(ln2_g[l]), row2(ln2_b[l]), alpha,
                           split_rows=xa.shape[0] if l == depth - 1 else None)
    return x


def kernel(x_prompt, x_sample, emb_ln_g, emb_ln_b, w_in, attn_sink, na_rpb, gnorm_a, gnorm_b, w_o, ln1_g, ln1_b,
           ffn_gate, ffn_up, ffn_down, router, exp_gate, exp_up, exp_down, ln2_g, ln2_b):
    groups = (x_prompt.shape[:2], x_sample.shape[:2])
    ta = x_prompt.shape[0] * x_prompt.shape[1]
    ya, yb = _trunk(x_prompt.reshape(ta, D_MODEL), x_sample.reshape(-1, D_MODEL), groups,
                    emb_ln_g, emb_ln_b, w_in, attn_sink, na_rpb, gnorm_a, gnorm_b, w_o, ln1_g, ln1_b,
                    ffn_gate, ffn_up, ffn_down, router, exp_gate, exp_up, exp_down, ln2_g, ln2_b)
    return ya.reshape(x_prompt.shape), yb.reshape(x_sample.shape)
```
